```python
import jax, jax.numpy as jnp
from jax import lax
import numpy as np

D_MODEL = 2048
BATCH = 1
SEQ = 8192
DEPTH = 1

CTX_LEN = 256
GRID_W = 64
D_MIX = D_MODEL
EPS = 1e-6
N_MOD = 6
CONV_CH = 1024
CONV_GROUPS = 8
CONV_WIDTH = 31
CONV_PAD = CONV_WIDTH // 2
N_HEADS = 16
Q_LORA = 512
KV_LORA = 256
QK_NOPE = 64
QK_ROPE = 32
V_DIM = 64
QK_DIM = QK_NOPE + QK_ROPE
ATTN_OUT = N_HEADS * V_DIM
SCALE = QK_DIM ** -0.5
ROPE_THETA = 10000.0
AXIS_ROPE = QK_ROPE // 2
AXIS_PAIRS = AXIS_ROPE // 2
Q_BLOCK = 128
Q0 = 2 * CONV_CH
KV0 = Q0 + Q_LORA
KR0 = KV0 + KV_LORA
IN_COLS = KR0 + QK_ROPE
N_EXPERTS = 32
TOP_K = 4
D_EXPERT = D_MODEL
SWIGLU_LIMIT = 7.0
SWIGLU_ALPHA = 1.702
MOE_BLOCK = 128

kernel_name = "hymba_conformer_mla_moe_dit_block"


def rms_norm(x, g):
    xf = x.astype(jnp.float32)
    y = xf * lax.rsqrt(jnp.mean(jnp.square(xf), axis=-1, keepdims=True) + EPS)
    return (y * g.astype(jnp.float32)).astype(x.dtype)


def modulate(h, shift, scale):
    return h * (1.0 + scale) + shift


def group_layer_norm(v, g, b):
    shp = v.shape
    vf = v.astype(jnp.float32).reshape(shp[:-1] + (CONV_GROUPS, CONV_CH // CONV_GROUPS))
    mu = jnp.mean(vf, axis=-1, keepdims=True)
    var = jnp.mean(jnp.square(vf - mu), axis=-1, keepdims=True)
    y = ((vf - mu) * lax.rsqrt(var + EPS)).reshape(shp)
    return (y * g.astype(jnp.float32) + b.astype(jnp.float32)).astype(v.dtype)


def conformer_conv(u, w, b, g, beta):
    v = u[..., :CONV_CH] * jax.nn.sigmoid(u[..., CONV_CH:])
    v = lax.conv_general_dilated(v, w[:, None, :].astype(v.dtype), window_strides=(1,),
                                 padding=[(CONV_PAD, CONV_PAD)],
                                 dimension_numbers=('NWC', 'WIO', 'NWC'),
                                 feature_group_count=CONV_CH) + b
    return jax.nn.silu(group_layer_norm(v, g, beta))


def axial_rope_tables(n_tokens):
    rows = n_tokens // GRID_W
    row = jnp.broadcast_to(jnp.arange(rows, dtype=jnp.float32)[:, None], (rows, GRID_W)).reshape(-1)
    col = jnp.broadcast_to(jnp.arange(GRID_W, dtype=jnp.float32)[None, :], (rows, GRID_W)).reshape(-1)
    inv_freq = ROPE_THETA ** (-2.0 * jnp.arange(AXIS_PAIRS, dtype=jnp.float32) / AXIS_ROPE)
    ang = jnp.stack([row[:, None] * inv_freq, col[:, None] * inv_freq], axis=1)
    return jnp.cos(ang), jnp.sin(ang)


def apply_axial_rope(r, cos, sin):
    shp = r.shape
    r = r.reshape(shp[:-1] + (2, 2, AXIS_PAIRS))
    a, b = r[..., 0, :], r[..., 1, :]
    cs = cos[None, :, None].astype(r.dtype)
    sn = sin[None, :, None].astype(r.dtype)
    out = jnp.stack([a * cs - b * sn, a * sn + b * cs], axis=-2)
    return out.reshape(shp)


def rope_tail(t, rope):
    if rope is None:
        return t
    cos, sin = rope
    return jnp.concatenate([t[..., :QK_NOPE], apply_axial_rope(t[..., QK_NOPE:], cos, sin)], axis=-1)


def mla_q(p_q, q_a_norm, w_q_up, q_head_norm, rope):
    bsz, s = p_q.shape[:2]
    q = (rms_norm(p_q, q_a_norm) @ w_q_up).reshape(bsz, s, N_HEADS, QK_DIM)
    return rope_tail(rms_norm(q, q_head_norm), rope)


def mla_kv(p_kv, p_kr, kv_a_norm, w_kv_up, k_head_norm, rope):
    bsz, s = p_kv.shape[:2]
    kv = (rms_norm(p_kv, kv_a_norm) @ w_kv_up).reshape(bsz, s, N_HEADS, QK_NOPE + V_DIM)
    k_rope = jnp.broadcast_to(p_kr[:, :, None, :], (bsz, s, N_HEADS, QK_ROPE))
    k = jnp.concatenate([kv[..., :QK_NOPE], k_rope], axis=-1)
    return rope_tail(rms_norm(k, k_head_norm), rope), kv[..., QK_NOPE:]


def attend(q, k, v):
    s = jnp.einsum('bqhd,bkhd->bhqk', q, k).astype(jnp.float32) * SCALE
    p = jax.nn.softmax(s, axis=-1).astype(v.dtype)
    return jnp.einsum('bhqk,bkhd->bqhd', p, v)


def latent_attention(q, k_lat, v_lat, k_ctx, v_ctx):
    bsz, s = q.shape[:2]
    k = jnp.concatenate([k_ctx, k_lat], axis=1)
    v = jnp.concatenate([v_ctx, v_lat], axis=1)
    qb = q.reshape(bsz, s // Q_BLOCK, Q_BLOCK, N_HEADS, QK_DIM).swapaxes(0, 1)
    o = lax.map(lambda qi: attend(qi, k, v), qb)
    return o.swapaxes(0, 1).reshape(bsz, s, ATTN_OUT)


def moe_ffn(h, router_w, router_b, w_gate_up, b_gate_up, w_down, b_down):
    t = h.shape[0]
    n_assign = t * TOP_K
    logits = h.astype(jnp.float32) @ router_w.astype(jnp.float32) + router_b.astype(jnp.float32)
    top_val, top_idx = lax.top_k(logits, TOP_K)
    gates = jax.nn.softmax(top_val, axis=-1)
    flat_e = top_idx.reshape(-1)
    order = jnp.argsort(flat_e)
    e_sorted = flat_e[order]
    tok_sorted = (order // TOP_K).astype(jnp.int32)
    w_sorted = gates.reshape(-1)[order]
    counts = jnp.bincount(flat_e, length=N_EXPERTS)
    padded = (counts + MOE_BLOCK - 1) // MOE_BLOCK * MOE_BLOCK
    pad_end = jnp.cumsum(padded)
    pad_start = pad_end - padded
    grp_start = jnp.cumsum(counts) - counts
    dest = pad_start[e_sorted] + (jnp.arange(n_assign) - grp_start[e_sorted])
    n_blocks = -(-n_assign // MOE_BLOCK) + N_EXPERTS
    n_rows = n_blocks * MOE_BLOCK
    row_tok = jnp.zeros((n_rows,), jnp.int32).at[dest].set(tok_sorted)
    block_e = jnp.minimum(jnp.searchsorted(pad_end, jnp.arange(n_blocks) * MOE_BLOCK, side='right'),
                          N_EXPERTS - 1)
    xb = h[row_tok].reshape(n_blocks, MOE_BLOCK, h.shape[-1])

    def expert_block(args):
        xi, e = args
        gu = xi @ w_gate_up[e] + b_gate_up[e]
        glu = jnp.minimum(gu[:, :D_EXPERT], SWIGLU_LIMIT)
        lin = jnp.clip(gu[:, D_EXPERT:], -SWIGLU_LIMIT, SWIGLU_LIMIT)
        return (glu * jax.nn.sigmoid(SWIGLU_ALPHA * glu) * (lin + 1.0)) @ w_down[e] + b_down[e]

    yb = lax.map(expert_block, (xb, block_e)).reshape(n_rows, h.shape[-1])
    contrib = yb[dest] * w_sorted[:, None].astype(h.dtype)
    return jnp.zeros_like(h).at[tok_sorted].add(contrib)


def setup_inputs(seed: int = 0) -> dict:
    key = jax.random.key(seed)
    ks = jax.random.split(key, 26)
    f32 = jnp.float32

    def nrm(k, shape, scale):
        return jax.random.normal(k, shape, f32) * scale

    def gain(k, shape):
        return 1.0 + 0.05 * jax.random.normal(k, shape, f32)

    L = DEPTH
    return {
        "x": nrm(ks[0], (BATCH, SEQ, D_MODEL), 1.0),
        "c": nrm(ks[1], (BATCH, D_MODEL), 1.0),
        "ctx": nrm(ks[2], (BATCH, CTX_LEN, D_MODEL), 1.0),
        "c_ctx": nrm(ks[3], (D_MODEL,), 1.0),
        "w_ada": nrm(ks[4], (L, D_MODEL, N_MOD * D_MODEL), 0.5 * D_MODEL ** -0.5),
        "b_ada": nrm(ks[5], (L, N_MOD * D_MODEL), 0.02),
        "norm_mix": gain(ks[6], (L, D_MODEL)),
        "norm_ffn": gain(ks[7], (L, D_MODEL)),
        "w_in": nrm(ks[8], (L, D_MODEL, IN_COLS), D_MODEL ** -0.5),
        "conv_w": nrm(ks[9], (L, CONV_WIDTH, CONV_CH), CONV_WIDTH ** -0.5),
        "conv_b": nrm(ks[10], (L, CONV_CH), 0.02),
        "conv_norm_g": gain(ks[11], (L, CONV_CH)),
        "conv_norm_b": nrm(ks[12], (L, CONV_CH), 0.02),
        "q_a_norm": gain(ks[13], (L, Q_LORA)),
        "w_q_up": nrm(ks[14], (L, Q_LORA, N_HEADS * QK_DIM), Q_LORA ** -0.5),
        "kv_a_norm": gain(ks[15], (L, KV_LORA)),
        "w_kv_up": nrm(ks[16], (L, KV_LORA, N_HEADS * (QK_NOPE + V_DIM)), KV_LORA ** -0.5),
        "q_head_norm": gain(ks[17], (L, QK_DIM)),
        "k_head_norm": gain(ks[18], (L, QK_DIM)),
        "w_out": nrm(ks[19], (L, D_MIX, D_MODEL), D_MIX ** -0.5),
        "router_w": nrm(ks[20], (L, D_MODEL, N_EXPERTS), D_MODEL ** -0.5),
        "router_b": nrm(ks[21], (L, N_EXPERTS), 0.01),
        "w_gate_up": nrm(ks[22], (L, N_EXPERTS, D_MODEL, 2 * D_EXPERT), D_MODEL ** -0.5),
        "b_gate_up": nrm(ks[23], (L, N_EXPERTS, 2 * D_EXPERT), 0.02),
        "w_down": nrm(ks[24], (L, N_EXPERTS, D_EXPERT, D_MODEL), D_EXPERT ** -0.5),
        "b_down": nrm(ks[25], (L, N_EXPERTS, D_MODEL), 0.02),
    }


def reference(x, c, ctx, c_ctx, w_ada, b_ada, norm_mix, norm_ffn, w_in, conv_w, conv_b,
              conv_norm_g, conv_norm_b, q_a_norm, w_q_up, kv_a_norm, w_kv_up, q_head_norm,
              k_head_norm, w_out, router_w, router_b, w_gate_up, b_gate_up, w_down, b_down):
    bsz, s, d = x.shape
    rope = axial_rope_tables(s)
    silu_c = jax.nn.silu(c)
    silu_cc = jax.nn.silu(c_ctx)[None]
    xc = ctx
    for l in range(DEPTH):
        mod = (silu_c @ w_ada[l] + b_ada[l])[:, None, :]
        mod_c = (silu_cc @ w_ada[l] + b_ada[l])[:, None, :]
        sh1, sc1, g1, sh2, sc2, g2 = jnp.split(mod, N_MOD, axis=-1)
        csh1, csc1, cg1, csh2, csc2, cg2 = jnp.split(mod_c, N_MOD, axis=-1)

        h = modulate(rms_norm(x, norm_mix[l]), sh1, sc1)
        hc = modulate(rms_norm(xc, norm_mix[l]), csh1, csc1)
        proj = h @ w_in[l]
        proj_c_kv = hc @ w_in[l][:, KV0:]

        q = mla_q(proj[..., Q0:KV0], q_a_norm[l], w_q_up[l], q_head_norm[l], rope)
        k, v = mla_kv(proj[..., KV0:KR0], proj[..., KR0:], kv_a_norm[l], w_kv_up[l], k_head_norm[l], rope)
        kc, vc = mla_kv(proj_c_kv[..., :KV_LORA], proj_c_kv[..., KV_LORA:], kv_a_norm[l], w_kv_up[l],
                        k_head_norm[l], None)
        attn = latent_attention(q, k, v, kc, vc)
        conv = conformer_conv(proj[..., :Q0], conv_w[l], conv_b[l], conv_norm_g[l], conv_norm_b[l])
        x = x + g1 * (jnp.concatenate([conv, attn], axis=-1) @ w_out[l])

        h2 = modulate(rms_norm(x, norm_ffn[l]), sh2, sc2)
        x = x + g2 * moe_ffn(h2.reshape(-1, d), router_w[l], router_b[l], w_gate_up[l], b_gate_up[l],
                             w_down[l], b_down[l]).reshape(x.shape)

        if l < DEPTH - 1:
            proj_c = hc @ w_in[l][:, :KV0]
            qc = mla_q(proj_c[..., Q0:], q_a_norm[l], w_q_up[l], q_head_norm[l], None)
            attn_c = attend(qc, kc, vc).reshape(bsz, xc.shape[1], ATTN_OUT)
            conv_c = conformer_conv(proj_c[..., :Q0], conv_w[l], conv_b[l], conv_norm_g[l], conv_norm_b[l])
            xc = xc + cg1 * (jnp.concatenate([conv_c, attn_c], axis=-1) @ w_out[l])
            h2c = modulate(rms_norm(xc, norm_ffn[l]), csh2, csc2)
            xc = xc + cg2 * moe_ffn(h2c.reshape(-1, d), router_w[l], router_b[l], w_gate_up[l], b_gate_up[l],
                                    w_down[l], b_down[l]).reshape(xc.shape)
    return x
```

```python
import functools
import math

import jax
import jax.numpy as jnp
from jax import lax
from jax.experimental import pallas as pl
from jax.experimental.pallas import tpu as pltpu

F32 = jnp.float32
BF16 = jnp.bfloat16

EPS = 1e-6
N_MOD = 6
GRID_W = 64
CONV_CH = 1024
CONV_GROUPS = 8
CONV_WIDTH = 31
N_HEADS = 16
Q_LORA = 512
KV_LORA = 256
QK_NOPE = 64
QK_ROPE = 32
V_DIM = 64
QK_DIM = QK_NOPE + QK_ROPE
SCALE = QK_DIM ** -0.5
ROPE_THETA = 10000.0
AXIS_ROPE = QK_ROPE // 2
AXIS_PAIRS = AXIS_ROPE // 2
Q0 = 2 * CONV_CH
KV0 = Q0 + Q_LORA
KR0 = KV0 + KV_LORA
N_EXPERTS = 32
TOP_K = 4
SWIGLU_LIMIT = 7.0
SWIGLU_ALPHA = 1.702

LANES = 128
HEAD_W = LANES
ROW_BLK = 128
VMEM_LIMIT = 56 * 1024 * 1024

_ROPE_PERM = tuple(ax * AXIS_ROPE + ab * AXIS_PAIRS + p
                   for ab in range(2) for ax in range(2) for p in range(AXIS_PAIRS))


def _cparams(sem, limit=VMEM_LIMIT):
    return pltpu.CompilerParams(dimension_semantics=sem, vmem_limit_bytes=limit)


def _const_spec(shape):
    nd = len(shape)
    return pl.BlockSpec(shape, lambda *_: (0,) * nd)


def _ada_kernel(c_ref, w_ref, b_ref, o_ref, *, rc):
    d, tn = w_ref.shape

    def body(k, carry):
        a0, a1 = carry
        r0 = pl.multiple_of(k * rc, rc)
        cc = c_ref[pl.ds(r0, rc), :]
        s = cc * jax.nn.sigmoid(cc)
        w = w_ref[pl.ds(r0, rc), :]
        p0 = (w * s[:, 0:1]).reshape(rc // 8, 8, tn).sum(axis=0)
        p1 = (w * s[:, 1:2]).reshape(rc // 8, 8, tn).sum(axis=0)
        return a0 + p0, a1 + p1

    z = jnp.zeros((8, tn), F32)
    a0, a1 = lax.fori_loop(0, d // rc, body, (z, z))
    b = b_ref[...]
    o_ref[0:1, :] = jnp.sum(a0, axis=0, keepdims=True) + b
    o_ref[1:2, :] = jnp.sum(a1, axis=0, keepdims=True) + b


def _ada(c2, w_ada, b_ada):
    d, n = w_ada.shape
    tn = 1024
    return pl.pallas_call(
        functools.partial(_ada_kernel, rc=128),
        out_shape=jax.ShapeDtypeStruct((2, n), F32),
        grid=(n // tn,),
        in_specs=[_const_spec((d, 2)),
                  pl.BlockSpec((d, tn), lambda j: (0, j)),
                  pl.BlockSpec((1, tn), lambda j: (0, j))],
        out_specs=pl.BlockSpec((2, tn), lambda j: (0, j)),
        compiler_params=_cparams(("arbitrary",)),
        name="ada",
    )(c2, w_ada, b_ada.reshape(1, n))


def _head_norm_rope(t, gain, cos, s_up, s_dn):
    r = lax.rsqrt(jnp.sum(t * t, axis=-1, keepdims=True) * (1.0 / QK_DIM) + EPS)
    t = t * r * gain
    return t * cos + pltpu.roll(t, AXIS_ROPE, 1) * s_up + pltpu.roll(t, HEAD_W - AXIS_ROPE, 1) * s_dn


def _inproj_kernel(x_ref, mod_ref, g_ref, win_ref, qa_ref, wq_ref, kva_ref, wkv_ref, gq_ref, gk_ref,
                   vone_ref, cos_ref, sup_ref, sdn_ref, glu_ref, q_ref, k_ref, v_ref):
    x = x_ref[...]
    ms = jnp.mean(x * x, axis=-1, keepdims=True)
    y = x * lax.rsqrt(ms + EPS) * g_ref[...]
    h = y * (1.0 + mod_ref[0, 1:2, :]) + mod_ref[0, 0:1, :]
    proj = jnp.dot(h.astype(BF16), win_ref[...], preferred_element_type=F32)
    glu_ref[...] = proj[:, :CONV_CH] * jax.nn.sigmoid(proj[:, CONV_CH:Q0])

    pq = proj[:, Q0:KV0]
    pq = pq * lax.rsqrt(jnp.mean(pq * pq, axis=-1, keepdims=True) + EPS) * qa_ref[...]
    q = jnp.dot(pq.astype(BF16), wq_ref[...], preferred_element_type=F32)
    pkv = proj[:, KV0:KR0]
    pkv = pkv * lax.rsqrt(jnp.mean(pkv * pkv, axis=-1, keepdims=True) + EPS) * kva_ref[...]
    kv = jnp.dot(pkv.astype(BF16), wkv_ref[...], preferred_element_type=F32)
    kr = proj[:, KR0:KR0 + HEAD_W]

    cos, s_up, s_dn = cos_ref[...], sup_ref[...], sdn_ref[...]
    gq, gk = gq_ref[...], gk_ref[...]
    for hd in range(N_HEADS):
        sl = slice(hd * HEAD_W, (hd + 1) * HEAD_W)
        q_ref[:, sl] = _head_norm_rope(q[:, sl], gq, cos, s_up, s_dn).astype(BF16)
        k_ref[:, sl] = _head_norm_rope(kv[:, sl] + kr, gk, cos, s_up, s_dn).astype(BF16)
    v_ref[...] = (kv[:, N_HEADS * HEAD_W:] + vone_ref[...]).astype(BF16)


def _inproj(xin, mod1, g_mix, win, qa, wq, kva, wkv, gq, gk, vone, cos, s_up, s_dn, n_lat, tm):
    t_all, d = xin.shape
    n_lat_tiles = n_lat // tm
    hw = N_HEADS * HEAD_W
    row = lambda w: pl.BlockSpec((tm, w), lambda i: (i, 0))
    return pl.pallas_call(
        _inproj_kernel,
        out_shape=(jax.ShapeDtypeStruct((t_all, CONV_CH), F32),
                   jax.ShapeDtypeStruct((t_all, hw), BF16),
                   jax.ShapeDtypeStruct((t_all, hw), BF16),
                   jax.ShapeDtypeStruct((t_all, hw), BF16)),
        grid=(t_all // tm,),
        in_specs=[row(d),
                  pl.BlockSpec((1, 2, d), lambda i: (jnp.where(i < n_lat_tiles, 0, 1), 0, 0)),
                  _const_spec(g_mix.shape), _const_spec(win.shape), _const_spec(qa.shape),
                  _const_spec(wq.shape), _const_spec(kva.shape), _const_spec(wkv.shape),
                  _const_spec(gq.shape), _const_spec(gk.shape), _const_spec(vone.shape),
                  row(HEAD_W), row(HEAD_W), row(HEAD_W)],
        out_specs=(row(CONV_CH), row(hw), row(hw), row(hw)),
        compiler_params=_cparams(("arbitrary",)),
        name="inproj",
    )(xin, mod1, g_mix, win, qa, wq, kva, wkv, gq, gk, vone, cos, s_up, s_dn)


CONV_HALO = 16
CONV_RC = 64


def _conv_kernel(prev_ref, main_ref, next_ref, w_ref, b_ref, g_ref, beta_ref, o_ref, buf_ref):
    i = pl.program_id(0)
    tc = main_ref.shape[0]
    gw = CONV_CH // CONV_GROUPS
    buf_ref[0:CONV_HALO, :] = jnp.where(i > 0, prev_ref[...], 0.0)
    buf_ref[CONV_HALO:CONV_HALO + tc, :] = main_ref[...]
    buf_ref[CONV_HALO + tc:, :] = jnp.where(i < pl.num_programs(0) - 1, next_ref[...], 0.0)
    win = CONV_RC + 2 * CONV_HALO

    def chunk(c, carry):
        r0 = pl.multiple_of(c * CONV_RC, CONV_RC)
        for g in range(CONV_GROUPS):
            ls = slice(g * gw, (g + 1) * gw)
            w_all = buf_ref[pl.ds(r0, win), ls]
            acc = jnp.zeros((CONV_RC, gw), F32)
            for b in range(8):
                wb = w_all[b:b + CONV_RC + 24, :]
                for a in range(4):
                    k = 8 * a + b - 1
                    if 0 <= k < CONV_WIDTH:
                        acc = acc + wb[8 * a:8 * a + CONV_RC, :] * w_ref[k:k + 1, ls]
            acc = acc + b_ref[:, ls]
            mu = jnp.mean(acc, axis=-1, keepdims=True)
            cen = acc - mu
            var = jnp.mean(cen * cen, axis=-1, keepdims=True)
            yv = cen * lax.rsqrt(var + EPS) * g_ref[:, ls] + beta_ref[:, ls]
            o_ref[pl.ds(r0, CONV_RC), ls] = (yv * jax.nn.sigmoid(yv)).astype(o_ref.dtype)
        return carry

    lax.fori_loop(0, tc // CONV_RC, chunk, 0)


def _conv(glu, w, b, g, beta, n_lat, tc):
    hb = tc // CONV_HALO
    return pl.pallas_call(
        _conv_kernel,
        out_shape=jax.ShapeDtypeStruct((n_lat, CONV_CH), BF16),
        grid=(n_lat // tc,),
        in_specs=[pl.BlockSpec((CONV_HALO, CONV_CH), lambda i: (jnp.maximum(i * hb - 1, 0), 0)),
                  pl.BlockSpec((tc, CONV_CH), lambda i: (i, 0)),
                  pl.BlockSpec((CONV_HALO, CONV_CH), lambda i: ((i + 1) * hb, 0)),
                  _const_spec(w.shape), _const_spec(b.shape), _const_spec(g.shape),
                  _const_spec(beta.shape)],
        out_specs=pl.BlockSpec((tc, CONV_CH), lambda i: (i, 0)),
        scratch_shapes=[pltpu.VMEM((tc + 2 * CONV_HALO, CONV_CH), F32)],
        compiler_params=_cparams(("arbitrary",)),
        name="conv",
    )(glu, glu, glu, w, b, g, beta)


def _attn_kernel(q_ref, k_ref, v_ref, o_ref, *, tk):
    tq = q_ref.shape[0]
    n_kv = k_ref.shape[0] // tk
    outs = []
    for hh in range(2):
        ls = slice(hh * HEAD_W, (hh + 1) * HEAD_W)
        q = q_ref[:, ls]

        def step(j, carry, ls=ls, q=q):
            m, acc = carry
            r0 = pl.multiple_of(j * tk, tk)
            kc = k_ref[pl.ds(r0, tk), ls]
            vc = v_ref[pl.ds(r0, tk), ls]
            s = lax.dot_general(q, kc, (((1,), (1,)), ((), ())), preferred_element_type=F32)
            m_new = jnp.maximum(m, jnp.max(s, axis=-1, keepdims=True))
            p = jnp.exp2(s - m_new)
            acc = acc * jnp.exp2(m - m_new) + jnp.dot(p.astype(BF16), vc, preferred_element_type=F32)
            return m_new, acc

        m0 = jnp.full((tq, 1), -jnp.inf, F32)
        _, acc = lax.fori_loop(0, n_kv, step, (m0, jnp.zeros((tq, HEAD_W), F32)))
        outs.append(acc[:, :V_DIM] / acc[:, V_DIM:V_DIM + 1])
    o_ref[...] = jnp.concatenate(outs, axis=-1).astype(o_ref.dtype)


def _attn(q, k, v, n_lat, tq, tk):
    t_all = k.shape[0]
    pw = 2 * HEAD_W
    return pl.pallas_call(
        functools.partial(_attn_kernel, tk=tk),
        out_shape=jax.ShapeDtypeStruct((n_lat, N_HEADS * V_DIM), BF16),
        grid=(N_HEADS // 2, n_lat // tq),
        in_specs=[pl.BlockSpec((tq, pw), lambda hp, i: (i, hp)),
                  pl.BlockSpec((t_all, pw), lambda hp, i: (0, hp)),
                  pl.BlockSpec((t_all, pw), lambda hp, i: (0, hp))],
        out_specs=pl.BlockSpec((tq, 2 * V_DIM), lambda hp, i: (i, hp)),
        compiler_params=_cparams(("arbitrary", "arbitrary")),
        name="attn",
    )(q, k, v)


def _outproj_kernel(conv_ref, attn_ref, x_ref, wo_ref, mod_ref, g_ref, rw_ref, rb_ref,
                    x1_ref, h2_ref, meta_ref, gate_ref, cnt_ref, cnt_acc):
    i = pl.program_id(0)
    tm = x_ref.shape[0]

    @pl.when(i == 0)
    def _():
        cnt_acc[...] = jnp.zeros_like(cnt_acc)

    y = jnp.dot(conv_ref[...], wo_ref[:CONV_CH, :], preferred_element_type=F32)
    y = y + jnp.dot(attn_ref[...], wo_ref[CONV_CH:, :], preferred_element_type=F32)
    x1 = x_ref[...] + mod_ref[0:1, :] * y
    x1_ref[...] = x1
    ms = jnp.mean(x1 * x1, axis=-1, keepdims=True)
    h2 = x1 * lax.rsqrt(ms + EPS) * g_ref[...]
    h2 = h2 * (1.0 + mod_ref[2:3, :]) + mod_ref[1:2, :]
    h2_ref[...] = h2

    h_hi = h2.astype(BF16)
    h_lo = (h2 - h_hi.astype(F32)).astype(BF16)
    rw = rw_ref[...]
    r_hi = rw.astype(BF16)
    r_lo = (rw - r_hi.astype(F32)).astype(BF16)
    logits = (jnp.dot(h_hi, r_hi, preferred_element_type=F32)
              + jnp.dot(h_lo, r_hi, preferred_element_type=F32)
              + jnp.dot(h_hi, r_lo, preferred_element_type=F32)) + rb_ref[...]

    lane = lax.broadcasted_iota(jnp.int32, (tm, LANES), 1)
    work = logits
    vals, idxs, sels = [], [], []
    for _ in range(TOP_K):
        mx = jnp.max(work, axis=-1, keepdims=True)
        idx = jnp.min(jnp.where(work == mx, lane, LANES), axis=-1, keepdims=True)
        sel = lane == idx
        vals.append(mx)
        idxs.append(idx)
        sels.append(sel)
        work = jnp.where(sel, -jnp.inf, work)
    es = [jnp.exp(v - vals[0]) for v in vals]
    denom = es[0] + es[1] + es[2] + es[3]

    chosen = jnp.zeros((tm, LANES), F32)
    for sel in sels:
        chosen = chosen + jnp.where(sel, 1.0, 0.0)
    rr = lax.broadcasted_iota(jnp.int32, (tm, tm), 0)
    cc = lax.broadcasted_iota(jnp.int32, (tm, tm), 1)
    tri = jnp.where(cc < rr, 1.0, 0.0).astype(BF16)
    before = jnp.dot(tri, chosen.astype(BF16), preferred_element_type=F32) + cnt_acc[...]

    meta = jnp.zeros((tm, LANES), jnp.int32)
    gates = jnp.zeros((tm, LANES), F32)
    for kk in range(TOP_K):
        pos = jnp.sum(jnp.where(sels[kk], before, 0.0), axis=-1, keepdims=True).astype(jnp.int32)
        meta = jnp.where(lane == kk, idxs[kk], meta)
        meta = jnp.where(lane == TOP_K + kk, pos, meta)
        gates = jnp.where(lane == kk, es[kk] / denom, gates)
    meta_ref[...] = meta
    gate_ref[...] = gates
    cnt_acc[...] = cnt_acc[...] + jnp.sum(chosen, axis=0, keepdims=True)
    cnt_ref[...] = cnt_acc[...]


def _outproj(conv, attn, x, wo, mod2, g_ffn, rw, rb, tm):
    n_lat, d = x.shape
    row = lambda w: pl.BlockSpec((tm, w), lambda i: (i, 0))
    return pl.pallas_call(
        _outproj_kernel,
        out_shape=(jax.ShapeDtypeStruct((n_lat, d), F32),
                   jax.ShapeDtypeStruct((n_lat, d), F32),
                   jax.ShapeDtypeStruct((n_lat, LANES), jnp.int32),
                   jax.ShapeDtypeStruct((n_lat, LANES), F32),
                   jax.ShapeDtypeStruct((1, LANES), F32)),
        grid=(n_lat // tm,),
        in_specs=[row(CONV_CH), row(N_HEADS * V_DIM), row(d), _const_spec(wo.shape),
                  _const_spec(mod2.shape), _const_spec(g_ffn.shape), _const_spec(rw.shape),
                  _const_spec(rb.shape)],
        out_specs=(row(d), row(d), row(LANES), row(LANES), _const_spec((1, LANES))),
        scratch_shapes=[pltpu.VMEM((1, LANES), F32)],
        compiler_params=_cparams(("arbitrary",)),
        name="outproj",
    )(conv, attn, x, wo, mod2, g_ffn, rw, rb)


def _gather_kernel(tok_ref, h_hbm, o_ref, buf, sem):
    nb = buf.shape[0]

    def row_copy(r):
        return pltpu.make_async_copy(h_hbm.at[pl.ds(tok_ref[0, 0, r], 1)], buf.at[pl.ds(r, 1)], sem)

    def issue(r, c):
        row_copy(r).start()
        return c

    lax.fori_loop(0, nb, issue, 0)

    def drain(r, c):
        row_copy(r).wait()
        return c

    lax.fori_loop(0, nb, drain, 0)
    o_ref[...] = buf[...].astype(o_ref.dtype)


def _gather_rows(row_tok, h2, nb):
    n_rows = row_tok.shape[0]
    d = h2.shape[1]
    return pl.pallas_call(
        _gather_kernel,
        out_shape=jax.ShapeDtypeStruct((n_rows, d), BF16),
        grid=(n_rows // nb,),
        in_specs=[pl.BlockSpec((1, 1, nb), lambda i: (i, 0, 0), memory_space=pltpu.SMEM),
                  pl.BlockSpec(memory_space=pl.ANY)],
        out_specs=pl.BlockSpec((nb, d), lambda i: (i, 0)),
        scratch_shapes=[pltpu.VMEM((nb, d), F32), pltpu.SemaphoreType.DMA],
        compiler_params=_cparams(("arbitrary",)),
        name="gather",
    )(row_tok.reshape(n_rows // nb, 1, nb), h2)


def _expert_kernel(be_ref, nu_ref, x_ref, wg_ref, wl_ref, wd_ref, bg_ref, bl_ref, base_ref, y_ref,
                   wgl_s, wd_s):
    b = pl.program_id(0)
    tf = wd_s.shape[0]

    @pl.when(b < nu_ref[0])
    def _():
        @pl.when((b == 0) | (be_ref[b] != be_ref[jnp.maximum(b - 1, 0)]))
        def _():
            wgl_s[:, :tf] = wg_ref[0].astype(BF16)
            wgl_s[:, tf:] = wl_ref[0].astype(BF16)
            wd_s[...] = wd_ref[0].astype(BF16)

        gu = jnp.dot(x_ref[...], wgl_s[...], preferred_element_type=F32)
        glu = jnp.minimum(gu[:, :tf] + bg_ref[0], SWIGLU_LIMIT)
        lin = jnp.clip(gu[:, tf:] + bl_ref[0], -SWIGLU_LIMIT, SWIGLU_LIMIT)
        hmid = glu * jax.nn.sigmoid(SWIGLU_ALPHA * glu) * (lin + 1.0)
        part = jnp.dot(hmid.astype(BF16), wd_s[...], preferred_element_type=F32)
        y_ref[...] = part + base_ref[...].reshape(-1, part.shape[-1])

    @pl.when(b >= nu_ref[0])
    def _():
        y_ref[...] = jnp.zeros_like(y_ref)


def _experts(block_e, n_used, xb, w_gate_up, b_gate_up, w_down, b_down, tf):
    n_rows, d = xb.shape
    n_blocks = n_rows // ROW_BLK
    de = w_down.shape[1]
    nf = de // tf
    n_e = w_down.shape[0]
    bgu = b_gate_up.reshape(n_e, 1, 2 * de)
    bdn = b_down.reshape(n_e, 1, d)
    blk = lambda b, nu: jnp.minimum(b, nu[0] - 1)
    y = None
    for f in range(nf):
        if f == 0:
            base, base_spec = bdn, pl.BlockSpec((1, 1, d), lambda b, be, nu: (be[blk(b, nu)], 0, 0))
        else:
            base, base_spec = y, pl.BlockSpec((ROW_BLK, d), lambda b, be, nu: (blk(b, nu), 0))
        grid_spec = pltpu.PrefetchScalarGridSpec(
            num_scalar_prefetch=2,
            grid=(n_blocks,),
            in_specs=[
                pl.BlockSpec((ROW_BLK, d), lambda b, be, nu: (blk(b, nu), 0)),
                pl.BlockSpec((1, d, tf), lambda b, be, nu, f=f: (be[blk(b, nu)], 0, f)),
                pl.BlockSpec((1, d, tf), lambda b, be, nu, f=f: (be[blk(b, nu)], 0, nf + f)),
                pl.BlockSpec((1, tf, d), lambda b, be, nu, f=f: (be[blk(b, nu)], f, 0)),
                pl.BlockSpec((1, 1, tf), lambda b, be, nu, f=f: (be[blk(b, nu)], 0, f)),
                pl.BlockSpec((1, 1, tf), lambda b, be, nu, f=f: (be[blk(b, nu)], 0, nf + f)),
                base_spec,
            ],
            out_specs=pl.BlockSpec((ROW_BLK, d), lambda b, be, nu: (b, 0)),
            scratch_shapes=[pltpu.VMEM((d, 2 * tf), BF16), pltpu.VMEM((tf, d), BF16)],
        )
        y = pl.pallas_call(
            _expert_kernel,
            out_shape=jax.ShapeDtypeStruct((n_rows, d), F32),
            grid_spec=grid_spec,
            compiler_params=_cparams(("arbitrary",)),
            name=f"experts{f}",
        )(block_e, n_used, xb, w_gate_up, w_gate_up, w_down, bgu, bgu, base)
    return y


def _combine_kernel(dest_ref, y_hbm, x1_ref, gate_ref, g2_ref, o_ref, buf, sem):
    tm = x1_ref.shape[0]
    n = TOP_K * tm

    def row_copy(r):
        return pltpu.make_async_copy(y_hbm.at[pl.ds(dest_ref[0, 0, r], 1)], buf.at[pl.ds(r, 1)], sem)

    def issue(r, c):
        row_copy(r).start()
        return c

    lax.fori_loop(0, n, issue, 0)

    def drain(r, c):
        row_copy(r).wait()
        return c

    lax.fori_loop(0, n, drain, 0)
    gates = gate_ref[...]
    acc = buf[0:tm, :] * gates[:, 0:1]
    for kk in range(1, TOP_K):
        acc = acc + buf[kk * tm:(kk + 1) * tm, :] * gates[:, kk:kk + 1]
    o_ref[...] = x1_ref[...] + g2_ref[...] * acc


def _combine(dest_km, yb, x1, gates, g2, tm):
    n_lat, d = x1.shape
    return pl.pallas_call(
        _combine_kernel,
        out_shape=jax.ShapeDtypeStruct((n_lat, d), F32),
        grid=(n_lat // tm,),
        in_specs=[pl.BlockSpec((1, 1, TOP_K * tm), lambda i: (i, 0, 0), memory_space=pltpu.SMEM),
                  pl.BlockSpec(memory_space=pl.ANY),
                  pl.BlockSpec((tm, d), lambda i: (i, 0)),
                  pl.BlockSpec((tm, LANES), lambda i: (i, 0)),
                  _const_spec((1, d))],
        out_specs=pl.BlockSpec((tm, d), lambda i: (i, 0)),
        scratch_shapes=[pltpu.VMEM((TOP_K * tm, d), F32), pltpu.SemaphoreType.DMA],
        compiler_params=_cparams(("arbitrary",)),
        name="combine",
    )(dest_km, yb, x1, gates, g2)


def _head_cols(w, per_head, start, width, perm=None):
    k = w.shape[0]
    t = w.reshape(k, N_HEADS, per_head)[:, :, start:start + width]
    return t if perm is None else t[:, :, jnp.asarray(perm)]


def _head_layout(nope, rope):
    k = nope.shape[0]
    parts = [nope]
    if rope is not None:
        parts.append(rope)
    used = sum(p.shape[-1] for p in parts)
    parts.append(jnp.zeros((k, N_HEADS, HEAD_W - used), nope.dtype))
    return jnp.concatenate(parts, axis=-1).reshape(k, N_HEADS * HEAD_W)


def _gain_row(g, mult=1.0):
    perm = jnp.asarray(_ROPE_PERM)
    return jnp.concatenate([g[:QK_NOPE], g[QK_NOPE:][perm], jnp.zeros((HEAD_W - QK_DIM,), F32)])[None] * mult


def _rope_tables(n_lat, n_ctx):
    rows = n_lat // GRID_W
    row = jnp.broadcast_to(jnp.arange(rows, dtype=F32)[:, None], (rows, GRID_W)).reshape(-1)
    col = jnp.broadcast_to(jnp.arange(GRID_W, dtype=F32)[None, :], (rows, GRID_W)).reshape(-1)
    inv_freq = ROPE_THETA ** (-2.0 * jnp.arange(AXIS_PAIRS, dtype=F32) / AXIS_ROPE)
    ang = jnp.concatenate([row[:, None] * inv_freq, col[:, None] * inv_freq], axis=1)
    cs, sn = jnp.cos(ang), jnp.sin(ang)
    one = jnp.ones((n_lat, QK_NOPE), F32)
    zn = jnp.zeros((n_lat, QK_NOPE), F32)
    zp = jnp.zeros((n_lat, HEAD_W - QK_DIM), F32)
    za = jnp.zeros((n_lat, AXIS_ROPE), F32)
    cos = jnp.concatenate([one, cs, cs, zp], axis=1)
    s_up = jnp.concatenate([zn, za, sn, zp], axis=1)
    s_dn = jnp.concatenate([zn, -sn, za, zp], axis=1)
    pad = lambda t, fill: jnp.concatenate([t, jnp.full((n_ctx, HEAD_W), fill, F32)], axis=0)
    return pad(cos, 1.0), pad(s_up, 0.0), pad(s_dn, 0.0)


def kernel(x, c, ctx, c_ctx, w_ada, b_ada, norm_mix, norm_ffn, w_in, conv_w, conv_b, conv_norm_g, conv_norm_b, q_a_norm, w_q_up, kv_a_norm, w_kv_up, q_head_norm, k_head_norm, w_out, router_w, router_b, w_gate_up, b_gate_up, w_down, b_down):
    bsz, n_lat, d = x.shape
    n_ctx = ctx.shape[1]
    assert bsz == 1 and w_ada.shape[0] == 1
    n_e = router_w.shape[-1]
    perm = _ROPE_PERM

    mod = _ada(jnp.stack([c[0], c_ctx], axis=1), w_ada[0], b_ada[0])
    mod = mod.reshape(2, N_MOD, d)
    mod1 = mod[:, 0:2, :]
    mod2 = jnp.stack([mod[0, 2], mod[0, 3], mod[0, 4]], axis=0)
    g2 = mod[0, 5][None]

    wi = w_in[0]
    kr_cols = jnp.concatenate([jnp.zeros((d, QK_NOPE), F32), wi[:, KR0:][:, jnp.asarray(perm)],
                               jnp.zeros((d, HEAD_W - QK_DIM), F32)], axis=1)
    win = jnp.concatenate([wi[:, :KR0], kr_cols], axis=1).astype(BF16)
    wq = _head_layout(_head_cols(w_q_up[0], QK_DIM, 0, QK_NOPE),
                      _head_cols(w_q_up[0], QK_DIM, QK_NOPE, QK_ROPE, perm)).astype(BF16)
    wk = _head_layout(_head_cols(w_kv_up[0], QK_NOPE + V_DIM, 0, QK_NOPE), None)
    wv = _head_layout(_head_cols(w_kv_up[0], QK_NOPE + V_DIM, QK_NOPE, V_DIM), None)
    wkv = jnp.concatenate([wk, wv], axis=1).astype(BF16)
    gq = _gain_row(q_head_norm[0], SCALE * math.log2(math.e))
    gk = _gain_row(k_head_norm[0])
    vone = jnp.zeros((N_HEADS, HEAD_W), F32).at[:, V_DIM].set(1.0).reshape(1, N_HEADS * HEAD_W)
    cos, s_up, s_dn = _rope_tables(n_lat, n_ctx)

    xin = jnp.concatenate([x[0], ctx[0]], axis=0)
    glu, q, k, v = _inproj(xin, mod1, norm_mix, win, q_a_norm, wq, kv_a_norm, wkv, gq, gk, vone,
                           cos, s_up, s_dn, n_lat, tm=256)
    conv = _conv(glu, conv_w[0], conv_b, conv_norm_g, conv_norm_b, n_lat, tc=512)
    t_all = n_lat + n_ctx
    tk = max(t for t in range(256, 1025, 256) if t_all % t == 0)
    attn = _attn(q, k, v, n_lat, tq=256, tk=tk)

    rw = jnp.concatenate([router_w[0], jnp.zeros((d, LANES - n_e), F32)], axis=1)
    rb = jnp.concatenate([router_b[0], jnp.full((LANES - n_e,), -1e30, F32)])[None]
    x1, h2, meta, gates, cnt = _outproj(conv, attn, x[0], w_out[0].astype(BF16), mod2, norm_ffn, rw, rb,
                                        tm=256)

    counts = cnt[0, :n_e].astype(jnp.int32)
    top_idx = meta[:, :TOP_K]
    pos = meta[:, TOP_K:2 * TOP_K]
    nblk_e = (counts + ROW_BLK - 1) // ROW_BLK
    blk_end = jnp.cumsum(nblk_e)
    pad_start = (blk_end - nblk_e) * ROW_BLK
    dest = pad_start[top_idx] + pos
    n_blocks = (n_lat * TOP_K) // ROW_BLK + n_e
    n_rows = n_blocks * ROW_BLK
    tok = jnp.broadcast_to(jnp.arange(n_lat, dtype=jnp.int32)[:, None], (n_lat, TOP_K))
    row_tok = jnp.zeros((n_rows,), jnp.int32).at[dest.reshape(-1)].set(tok.reshape(-1))
    block_e = jnp.minimum(jnp.searchsorted(blk_end, jnp.arange(n_blocks), side='right'),
                          n_e - 1).astype(jnp.int32)
    n_used = blk_end[-1:].astype(jnp.int32)

    xb = _gather_rows(row_tok, h2, nb=256)
    yb = _experts(block_e, n_used, xb, w_gate_up[0], b_gate_up[0], w_down[0], b_down[0], tf=512)

    tm_c = 128
    dest_km = dest.reshape(n_lat // tm_c, tm_c, TOP_K).transpose(0, 2, 1).reshape(n_lat // tm_c, 1,
                                                                                  TOP_K * tm_c)
    out = _combine(dest_km, yb, x1, gates, g2, tm=tm_c)
    return out[None]
```

```python
import functools
import math

import jax
import jax.numpy as jnp
from jax import lax
from jax.experimental import pallas as pl
from jax.experimental.pallas import tpu as pltpu

F32 = jnp.float32
BF16 = jnp.bfloat16

EPS = 1e-6
N_MOD = 6
GRID_W = 64
CONV_CH = 1024
CONV_GROUPS = 8
CONV_WIDTH = 31
N_HEADS = 16
Q_LORA = 512
KV_LORA = 256
QK_NOPE = 64
QK_ROPE = 32
V_DIM = 64
QK_DIM = QK_NOPE + QK_ROPE
SCALE = QK_DIM ** -0.5
ROPE_THETA = 10000.0
AXIS_ROPE = QK_ROPE // 2
AXIS_PAIRS = AXIS_ROPE // 2
Q0 = 2 * CONV_CH
KV0 = Q0 + Q_LORA
KR0 = KV0 + KV_LORA
N_EXPERTS = 32
TOP_K = 4
SWIGLU_LIMIT = 7.0
SWIGLU_ALPHA = 1.702

LANES = 128
HEAD_W = LANES
ROW_BLK = 128
VMEM_LIMIT = 56 * 1024 * 1024

_ROPE_PERM = tuple(ax * AXIS_ROPE + ab * AXIS_PAIRS + p
                   for ab in range(2) for ax in range(2) for p in range(AXIS_PAIRS))


def _cparams(sem, limit=VMEM_LIMIT):
    return pltpu.CompilerParams(dimension_semantics=sem, vmem_limit_bytes=limit)


def _const_spec(shape):
    nd = len(shape)
    return pl.BlockSpec(shape, lambda *_: (0,) * nd)


def _ada_kernel(c_ref, w_ref, b_ref, o_ref, *, rc):
    d, tn = w_ref.shape

    def body(k, carry):
        a0, a1 = carry
        r0 = pl.multiple_of(k * rc, rc)
        cc = c_ref[pl.ds(r0, rc), :]
        s = cc * jax.nn.sigmoid(cc)
        w = w_ref[pl.ds(r0, rc), :]
        p0 = (w * s[:, 0:1]).reshape(rc // 8, 8, tn).sum(axis=0)
        p1 = (w * s[:, 1:2]).reshape(rc // 8, 8, tn).sum(axis=0)
        return a0 + p0, a1 + p1

    z = jnp.zeros((8, tn), F32)
    a0, a1 = lax.fori_loop(0, d // rc, body, (z, z))
    b = b_ref[...]
    o_ref[0:1, :] = jnp.sum(a0, axis=0, keepdims=True) + b
    o_ref[1:2, :] = jnp.sum(a1, axis=0, keepdims=True) + b


def _ada(c2, w_ada, b_ada):
    d, n = w_ada.shape
    tn = 1024
    return pl.pallas_call(
        functools.partial(_ada_kernel, rc=128),
        out_shape=jax.ShapeDtypeStruct((2, n), F32),
        grid=(n // tn,),
        in_specs=[_const_spec((d, 2)),
                  pl.BlockSpec((d, tn), lambda j: (0, j)),
                  pl.BlockSpec((1, tn), lambda j: (0, j))],
        out_specs=pl.BlockSpec((2, tn), lambda j: (0, j)),
        compiler_params=_cparams(("arbitrary",)),
        name="ada",
    )(c2, w_ada, b_ada.reshape(1, n))


def _head_norm_rope(t, gain, cos, s_up, s_dn):
    r = lax.rsqrt(jnp.sum(t * t, axis=-1, keepdims=True) * (1.0 / QK_DIM) + EPS)
    t = t * r * gain
    return t * cos + pltpu.roll(t, AXIS_ROPE, 1) * s_up + pltpu.roll(t, HEAD_W - AXIS_ROPE, 1) * s_dn


def _inproj_kernel(x_ref, mod_ref, g_ref, win_ref, qa_ref, wq_ref, kva_ref, wkv_ref, gq_ref, gk_ref,
                   qadd_ref, kadd_ref, vone_ref, cos_ref, sup_ref, sdn_ref, glu_ref, q_ref, k_ref, vt_ref):
    x = x_ref[...]
    ms = jnp.mean(x * x, axis=-1, keepdims=True)
    y = x * lax.rsqrt(ms + EPS) * g_ref[...]
    h = y * (1.0 + mod_ref[0, 1:2, :]) + mod_ref[0, 0:1, :]
    proj = jnp.dot(h.astype(BF16), win_ref[...], preferred_element_type=F32)
    glu_ref[...] = proj[:, :CONV_CH] * jax.nn.sigmoid(proj[:, CONV_CH:Q0])

    pq = proj[:, Q0:KV0]
    pq = pq * lax.rsqrt(jnp.mean(pq * pq, axis=-1, keepdims=True) + EPS) * qa_ref[...]
    q = jnp.dot(pq.astype(BF16), wq_ref[...], preferred_element_type=F32)
    pkv = proj[:, KV0:KR0]
    pkv = pkv * lax.rsqrt(jnp.mean(pkv * pkv, axis=-1, keepdims=True) + EPS) * kva_ref[...]
    kv = jnp.dot(pkv.astype(BF16), wkv_ref[...], preferred_element_type=F32)
    kr = proj[:, KR0:KR0 + HEAD_W]

    cos, s_up, s_dn = cos_ref[...], sup_ref[...], sdn_ref[...]
    gq, gk = gq_ref[...], gk_ref[...]
    qadd, kadd = qadd_ref[...], kadd_ref[...]
    for hd in range(N_HEADS):
        sl = slice(hd * HEAD_W, (hd + 1) * HEAD_W)
        q_ref[:, sl] = (_head_norm_rope(q[:, sl], gq, cos, s_up, s_dn) + qadd).astype(BF16)
        k_ref[:, sl] = (_head_norm_rope(kv[:, sl] + kr, gk, cos, s_up, s_dn) + kadd).astype(BF16)
    vt_ref[...] = (kv[:, N_HEADS * HEAD_W:] + vone_ref[...]).T.astype(BF16)


def _inproj(xin, mod1, g_mix, win, qa, wq, kva, wkv, gq, gk, qadd, kadd, vone, cos, s_up, s_dn, n_lat, tm):
    t_all, d = xin.shape
    n_lat_tiles = n_lat // tm
    hw = N_HEADS * HEAD_W
    row = lambda w: pl.BlockSpec((tm, w), lambda i: (i, 0))
    return pl.pallas_call(
        _inproj_kernel,
        out_shape=(jax.ShapeDtypeStruct((t_all, CONV_CH), F32),
                   jax.ShapeDtypeStruct((t_all, hw), BF16),
                   jax.ShapeDtypeStruct((t_all, hw), BF16),
                   jax.ShapeDtypeStruct((hw, t_all), BF16)),
        grid=(t_all // tm,),
        in_specs=[row(d),
                  pl.BlockSpec((1, 2, d), lambda i: (jnp.where(i < n_lat_tiles, 0, 1), 0, 0)),
                  _const_spec(g_mix.shape), _const_spec(win.shape), _const_spec(qa.shape),
                  _const_spec(wq.shape), _const_spec(kva.shape), _const_spec(wkv.shape),
                  _const_spec(gq.shape), _const_spec(gk.shape), _const_spec(qadd.shape),
                  _const_spec(kadd.shape), _const_spec(vone.shape),
                  row(HEAD_W), row(HEAD_W), row(HEAD_W)],
        out_specs=(row(CONV_CH), row(hw), row(hw), pl.BlockSpec((hw, tm), lambda i: (0, i))),
        compiler_params=_cparams(("arbitrary",)),
        name="inproj",
    )(xin, mod1, g_mix, win, qa, wq, kva, wkv, gq, gk, qadd, kadd, vone, cos, s_up, s_dn)


CONV_HALO = 16
CONV_RC = 64


def _conv_kernel(prev_ref, main_ref, next_ref, w_ref, b_ref, g_ref, beta_ref, o_ref, buf_ref):
    i = pl.program_id(0)
    tc = main_ref.shape[0]
    gw = CONV_CH // CONV_GROUPS
    buf_ref[0:CONV_HALO, :] = jnp.where(i > 0, prev_ref[...], 0.0)
    buf_ref[CONV_HALO:CONV_HALO + tc, :] = main_ref[...]
    buf_ref[CONV_HALO + tc:, :] = jnp.where(i < pl.num_programs(0) - 1, next_ref[...], 0.0)
    win = CONV_RC + 2 * CONV_HALO

    def chunk(c, carry):
        r0 = pl.multiple_of(c * CONV_RC, CONV_RC)
        for g in range(CONV_GROUPS):
            ls = slice(g * gw, (g + 1) * gw)
            w_all = buf_ref[pl.ds(r0, win), ls]
            acc = jnp.zeros((CONV_RC, gw), F32)
            for b in range(8):
                wb = w_all[b:b + CONV_RC + 24, :]
                for a in range(4):
                    k = 8 * a + b - 1
                    if 0 <= k < CONV_WIDTH:
                        acc = acc + wb[8 * a:8 * a + CONV_RC, :] * w_ref[k:k + 1, ls]
            acc = acc + b_ref[:, ls]
            mu = jnp.mean(acc, axis=-1, keepdims=True)
            cen = acc - mu
            var = jnp.mean(cen * cen, axis=-1, keepdims=True)
            yv = cen * lax.rsqrt(var + EPS) * g_ref[:, ls] + beta_ref[:, ls]
            o_ref[pl.ds(r0, CONV_RC), ls] = (yv * jax.nn.sigmoid(yv)).astype(o_ref.dtype)
        return carry

    lax.fori_loop(0, tc // CONV_RC, chunk, 0)


def _conv(glu, w, b, g, beta, n_lat, tc):
    hb = tc // CONV_HALO
    return pl.pallas_call(
        _conv_kernel,
        out_shape=jax.ShapeDtypeStruct((n_lat, CONV_CH), BF16),
        grid=(n_lat // tc,),
        in_specs=[pl.BlockSpec((CONV_HALO, CONV_CH), lambda i: (jnp.maximum(i * hb - 1, 0), 0)),
                  pl.BlockSpec((tc, CONV_CH), lambda i: (i, 0)),
                  pl.BlockSpec((CONV_HALO, CONV_CH), lambda i: ((i + 1) * hb, 0)),
                  _const_spec(w.shape), _const_spec(b.shape), _const_spec(g.shape),
                  _const_spec(beta.shape)],
        out_specs=pl.BlockSpec((tc, CONV_CH), lambda i: (i, 0)),
        scratch_shapes=[pltpu.VMEM((tc + 2 * CONV_HALO, CONV_CH), F32)],
        compiler_params=_cparams(("arbitrary",)),
        name="conv",
    )(glu, glu, glu, w, b, g, beta)


ATTN_HEADS_PER_STEP = 2
SHIFT_LIMIT = 40.0


def _attn_fixed_kernel(q_ref, k_ref, vt_ref, o_ref, *, tk):
    tq = q_ref.shape[0]
    n_kv = k_ref.shape[0] // tk
    qs = [q_ref[:, hh * HEAD_W:(hh + 1) * HEAD_W] for hh in range(ATTN_HEADS_PER_STEP)]

    def step(j, carry):
        r0 = pl.multiple_of(j * tk, tk)
        out = []
        for hh in range(ATTN_HEADS_PER_STEP):
            kc = k_ref[pl.ds(r0, tk), hh * HEAD_W:(hh + 1) * HEAD_W]
            vtc = vt_ref[hh * HEAD_W:(hh + 1) * HEAD_W, pl.ds(r0, tk)]
            st = lax.dot_general(kc, qs[hh], (((1,), (1,)), ((), ())), preferred_element_type=F32)
            pt = jnp.exp2(st).astype(BF16)
            out.append(carry[hh] + jnp.dot(vtc, pt, preferred_element_type=F32))
        return tuple(out)

    init = tuple(jnp.zeros((HEAD_W, tq), F32) for _ in range(ATTN_HEADS_PER_STEP))
    res = lax.fori_loop(0, n_kv, step, init)
    outs = [(acc[:V_DIM, :] / acc[V_DIM:V_DIM + 1, :]).T for acc in res]
    o_ref[...] = jnp.concatenate(outs, axis=-1).astype(o_ref.dtype)


def _attn_online_kernel(q_ref, k_ref, vt_ref, o_ref, *, tk):
    tq = q_ref.shape[0]
    n_kv = k_ref.shape[0] // tk
    qs = [q_ref[:, hh * HEAD_W:(hh + 1) * HEAD_W] for hh in range(ATTN_HEADS_PER_STEP)]

    def step(j, carry):
        r0 = pl.multiple_of(j * tk, tk)
        out = []
        for hh in range(ATTN_HEADS_PER_STEP):
            m, acc = carry[hh]
            kc = k_ref[pl.ds(r0, tk), hh * HEAD_W:(hh + 1) * HEAD_W]
            vtc = vt_ref[hh * HEAD_W:(hh + 1) * HEAD_W, pl.ds(r0, tk)]
            st = lax.dot_general(kc, qs[hh], (((1,), (1,)), ((), ())), preferred_element_type=F32)
            m_new = jnp.maximum(m, jnp.max(st, axis=0, keepdims=True))
            pt = jnp.exp2(st - m_new).astype(BF16)
            acc = acc * jnp.exp2(m - m_new) + jnp.dot(vtc, pt, preferred_element_type=F32)
            out.append((m_new, acc))
        return tuple(out)

    init = tuple((jnp.full((1, tq), -jnp.inf, F32), jnp.zeros((HEAD_W, tq), F32))
                 for _ in range(ATTN_HEADS_PER_STEP))
    res = lax.fori_loop(0, n_kv, step, init)
    outs = [(acc[:V_DIM, :] / acc[V_DIM:V_DIM + 1, :]).T for _, acc in res]
    o_ref[...] = jnp.concatenate(outs, axis=-1).astype(o_ref.dtype)


def _attn(body, q, k, vt, n_lat, tq, tk):
    t_all = k.shape[0]
    pw = ATTN_HEADS_PER_STEP * HEAD_W
    return pl.pallas_call(
        functools.partial(body, tk=tk),
        out_shape=jax.ShapeDtypeStruct((n_lat, N_HEADS * V_DIM), BF16),
        grid=(N_HEADS // ATTN_HEADS_PER_STEP, n_lat // tq),
        in_specs=[pl.BlockSpec((tq, pw), lambda hp, i: (i, hp)),
                  pl.BlockSpec((t_all, pw), lambda hp, i: (0, hp)),
                  pl.BlockSpec((pw, t_all), lambda hp, i: (hp, 0))],
        out_specs=pl.BlockSpec((tq, ATTN_HEADS_PER_STEP * V_DIM), lambda hp, i: (i, hp)),
        compiler_params=_cparams(("arbitrary", "arbitrary")),
        name="attn",
    )(q, k, vt)


def _outproj_kernel(conv_ref, attn_ref, x_ref, wo_ref, mod_ref, g_ref, rw_ref, rb_ref,
                    x1_ref, h2_ref, meta_ref, gate_ref, cnt_ref, cnt_acc):
    i = pl.program_id(0)
    tm = x_ref.shape[0]

    @pl.when(i == 0)
    def _():
        cnt_acc[...] = jnp.zeros_like(cnt_acc)

    y = jnp.dot(conv_ref[...], wo_ref[:CONV_CH, :], preferred_element_type=F32)
    y = y + jnp.dot(attn_ref[...], wo_ref[CONV_CH:, :], preferred_element_type=F32)
    x1 = x_ref[...] + mod_ref[0:1, :] * y
    x1_ref[...] = x1
    ms = jnp.mean(x1 * x1, axis=-1, keepdims=True)
    h2 = x1 * lax.rsqrt(ms + EPS) * g_ref[...]
    h2 = h2 * (1.0 + mod_ref[2:3, :]) + mod_ref[1:2, :]
    h2_ref[...] = h2

    h_hi = h2.astype(BF16)
    h_lo = (h2 - h_hi.astype(F32)).astype(BF16)
    rw = rw_ref[...]
    r_hi = rw.astype(BF16)
    r_lo = (rw - r_hi.astype(F32)).astype(BF16)
    logits = (jnp.dot(h_hi, r_hi, preferred_element_type=F32)
              + jnp.dot(h_lo, r_hi, preferred_element_type=F32)
              + jnp.dot(h_hi, r_lo, preferred_element_type=F32)) + rb_ref[...]

    lane = lax.broadcasted_iota(jnp.int32, (tm, LANES), 1)
    work = logits
    vals, idxs, sels = [], [], []
    for _ in range(TOP_K):
        mx = jnp.max(work, axis=-1, keepdims=True)
        idx = jnp.min(jnp.where(work == mx, lane, LANES), axis=-1, keepdims=True)
        sel = lane == idx
        vals.append(mx)
        idxs.append(idx)
        sels.append(sel)
        work = jnp.where(sel, -jnp.inf, work)
    es = [jnp.exp(v - vals[0]) for v in vals]
    denom = es[0] + es[1] + es[2] + es[3]

    chosen = jnp.zeros((tm, LANES), F32)
    for sel in sels:
        chosen = chosen + jnp.where(sel, 1.0, 0.0)
    rr = lax.broadcasted_iota(jnp.int32, (tm, tm), 0)
    cc = lax.broadcasted_iota(jnp.int32, (tm, tm), 1)
    tri = jnp.where(cc < rr, 1.0, 0.0).astype(BF16)
    before = jnp.dot(tri, chosen.astype(BF16), preferred_element_type=F32) + cnt_acc[...]

    meta = jnp.zeros((tm, LANES), jnp.int32)
    gates = jnp.zeros((tm, LANES), F32)
    for kk in range(TOP_K):
        pos = jnp.sum(jnp.where(sels[kk], before, 0.0), axis=-1, keepdims=True).astype(jnp.int32)
        meta = jnp.where(lane == kk, idxs[kk], meta)
        meta = jnp.where(lane == TOP_K + kk, pos, meta)
        gates = jnp.where(lane == kk, es[kk] / denom, gates)
    meta_ref[...] = meta
    gate_ref[...] = gates
    cnt_acc[...] = cnt_acc[...] + jnp.sum(chosen, axis=0, keepdims=True)
    cnt_ref[...] = cnt_acc[...]


def _outproj(conv, attn, x, wo, mod2, g_ffn, rw, rb, tm):
    n_lat, d = x.shape
    row = lambda w: pl.BlockSpec((tm, w), lambda i: (i, 0))
    return pl.pallas_call(
        _outproj_kernel,
        out_shape=(jax.ShapeDtypeStruct((n_lat, d), F32),
                   jax.ShapeDtypeStruct((n_lat, d), F32),
                   jax.ShapeDtypeStruct((n_lat, LANES), jnp.int32),
                   jax.ShapeDtypeStruct((n_lat, LANES), F32),
                   jax.ShapeDtypeStruct((1, LANES), F32)),
        grid=(n_lat // tm,),
        in_specs=[row(CONV_CH), row(N_HEADS * V_DIM), row(d), _const_spec(wo.shape),
                  _const_spec(mod2.shape), _const_spec(g_ffn.shape), _const_spec(rw.shape),
                  _const_spec(rb.shape)],
        out_specs=(row(d), row(d), row(LANES), row(LANES), _const_spec((1, LANES))),
        scratch_shapes=[pltpu.VMEM((1, LANES), F32)],
        compiler_params=_cparams(("arbitrary",)),
        name="outproj",
    )(conv, attn, x, wo, mod2, g_ffn, rw, rb)


def _gather_kernel(tok_ref, h_hbm, o_ref, buf, sem):
    nb = buf.shape[0]

    def row_copy(r):
        return pltpu.make_async_copy(h_hbm.at[pl.ds(tok_ref[0, 0, r], 1)], buf.at[pl.ds(r, 1)], sem)

    def issue(r, c):
        row_copy(r).start()
        return c

    lax.fori_loop(0, nb, issue, 0)

    def drain(r, c):
        row_copy(r).wait()
        return c

    lax.fori_loop(0, nb, drain, 0)
    o_ref[...] = buf[...].astype(o_ref.dtype)


def _gather_rows(row_tok, h2, nb):
    n_rows = row_tok.shape[0]
    d = h2.shape[1]
    return pl.pallas_call(
        _gather_kernel,
        out_shape=jax.ShapeDtypeStruct((n_rows, d), BF16),
        grid=(n_rows // nb,),
        in_specs=[pl.BlockSpec((1, 1, nb), lambda i: (i, 0, 0), memory_space=pltpu.SMEM),
                  pl.BlockSpec(memory_space=pl.ANY)],
        out_specs=pl.BlockSpec((nb, d), lambda i: (i, 0)),
        scratch_shapes=[pltpu.VMEM((nb, d), F32), pltpu.SemaphoreType.DMA],
        compiler_params=_cparams(("arbitrary",)),
        name="gather",
    )(row_tok.reshape(n_rows // nb, 1, nb), h2)


def _expert_kernel(be_ref, nu_ref, x_ref, wg_ref, wl_ref, wd_ref, bg_ref, bl_ref, base_ref, y_ref,
                   wgl_s, wd_s):
    b = pl.program_id(0)
    tf = wd_s.shape[0]

    @pl.when(b < nu_ref[0])
    def _():
        @pl.when((b == 0) | (be_ref[b] != be_ref[jnp.maximum(b - 1, 0)]))
        def _():
            wgl_s[:, :tf] = wg_ref[0].astype(BF16)
            wgl_s[:, tf:] = wl_ref[0].astype(BF16)
            wd_s[...] = wd_ref[0].astype(BF16)

        gu = jnp.dot(x_ref[...], wgl_s[...], preferred_element_type=F32)
        glu = jnp.minimum(gu[:, :tf] + bg_ref[0], SWIGLU_LIMIT)
        lin = jnp.clip(gu[:, tf:] + bl_ref[0], -SWIGLU_LIMIT, SWIGLU_LIMIT)
        hmid = glu * jax.nn.sigmoid(SWIGLU_ALPHA * glu) * (lin + 1.0)
        part = jnp.dot(hmid.astype(BF16), wd_s[...], preferred_element_type=F32)
        y_ref[...] = part + base_ref[...].reshape(-1, part.shape[-1])

    @pl.when(b >= nu_ref[0])
    def _():
        y_ref[...] = jnp.zeros_like(y_ref)


def _experts(block_e, n_used, xb, w_gate_up, b_gate_up, w_down, b_down, tf):
    n_rows, d = xb.shape
    n_blocks = n_rows // ROW_BLK
    de = w_down.shape[1]
    nf = de // tf
    n_e = w_down.shape[0]
    bgu = b_gate_up.reshape(n_e, 1, 2 * de)
    bdn = b_down.reshape(n_e, 1, d)
    blk = lambda b, nu: jnp.minimum(b, nu[0] - 1)
    y = None
    for f in range(nf):
        if f == 0:
            base, base_spec = bdn, pl.BlockSpec((1, 1, d), lambda b, be, nu: (be[blk(b, nu)], 0, 0))
        else:
            base, base_spec = y, pl.BlockSpec((ROW_BLK, d), lambda b, be, nu: (blk(b, nu), 0))
        grid_spec = pltpu.PrefetchScalarGridSpec(
            num_scalar_prefetch=2,
            grid=(n_blocks,),
            in_specs=[
                pl.BlockSpec((ROW_BLK, d), lambda b, be, nu: (blk(b, nu), 0)),
                pl.BlockSpec((1, d, tf), lambda b, be, nu, f=f: (be[blk(b, nu)], 0, f)),
                pl.BlockSpec((1, d, tf), lambda b, be, nu, f=f: (be[blk(b, nu)], 0, nf + f)),
                pl.BlockSpec((1, tf, d), lambda b, be, nu, f=f: (be[blk(b, nu)], f, 0)),
                pl.BlockSpec((1, 1, tf), lambda b, be, nu, f=f: (be[blk(b, nu)], 0, f)),
                pl.BlockSpec((1, 1, tf), lambda b, be, nu, f=f: (be[blk(b, nu)], 0, nf + f)),
                base_spec,
            ],
            out_specs=pl.BlockSpec((ROW_BLK, d), lambda b, be, nu: (b, 0)),
            scratch_shapes=[pltpu.VMEM((d, 2 * tf), BF16), pltpu.VMEM((tf, d), BF16)],
        )
        y = pl.pallas_call(
            _expert_kernel,
            out_shape=jax.ShapeDtypeStruct((n_rows, d), F32),
            grid_spec=grid_spec,
            compiler_params=_cparams(("arbitrary",)),
            name=f"experts{f}",
        )(block_e, n_used, xb, w_gate_up, w_gate_up, w_down, bgu, bgu, base)
    return y


def _combine_kernel(dest_ref, y_hbm, x1_ref, gate_ref, g2_ref, o_ref, buf, sem):
    tm = x1_ref.shape[0]
    n = TOP_K * tm

    def row_copy(r):
        return pltpu.make_async_copy(y_hbm.at[pl.ds(dest_ref[0, 0, r], 1)], buf.at[pl.ds(r, 1)], sem)

    def issue(r, c):
        row_copy(r).start()
        return c

    lax.fori_loop(0, n, issue, 0)

    def drain(r, c):
        row_copy(r).wait()
        return c

    lax.fori_loop(0, n, drain, 0)
    gates = gate_ref[...]
    acc = buf[0:tm, :] * gates[:, 0:1]
    for kk in range(1, TOP_K):
        acc = acc + buf[kk * tm:(kk + 1) * tm, :] * gates[:, kk:kk + 1]
    o_ref[...] = x1_ref[...] + g2_ref[...] * acc


def _combine(dest_km, yb, x1, gates, g2, tm):
    n_lat, d = x1.shape
    return pl.pallas_call(
        _combine_kernel,
        out_shape=jax.ShapeDtypeStruct((n_lat, d), F32),
        grid=(n_lat // tm,),
        in_specs=[pl.BlockSpec((1, 1, TOP_K * tm), lambda i: (i, 0, 0), memory_space=pltpu.SMEM),
                  pl.BlockSpec(memory_space=pl.ANY),
                  pl.BlockSpec((tm, d), lambda i: (i, 0)),
                  pl.BlockSpec((tm, LANES), lambda i: (i, 0)),
                  _const_spec((1, d))],
        out_specs=pl.BlockSpec((tm, d), lambda i: (i, 0)),
        scratch_shapes=[pltpu.VMEM((TOP_K * tm, d), F32), pltpu.SemaphoreType.DMA],
        compiler_params=_cparams(("arbitrary",)),
        name="combine",
    )(dest_km, yb, x1, gates, g2)


def _head_cols(w, per_head, start, width, perm=None):
    k = w.shape[0]
    t = w.reshape(k, N_HEADS, per_head)[:, :, start:start + width]
    return t if perm is None else t[:, :, jnp.asarray(perm)]


def _head_layout(nope, rope):
    k = nope.shape[0]
    parts = [nope]
    if rope is not None:
        parts.append(rope)
    used = sum(p.shape[-1] for p in parts)
    parts.append(jnp.zeros((k, N_HEADS, HEAD_W - used), nope.dtype))
    return jnp.concatenate(parts, axis=-1).reshape(k, N_HEADS * HEAD_W)


def _gain_row(g, mult=1.0):
    perm = jnp.asarray(_ROPE_PERM)
    return jnp.concatenate([g[:QK_NOPE], g[QK_NOPE:][perm], jnp.zeros((HEAD_W - QK_DIM,), F32)])[None] * mult


def _rope_tables(n_lat, n_ctx):
    rows = n_lat // GRID_W
    row = jnp.broadcast_to(jnp.arange(rows, dtype=F32)[:, None], (rows, GRID_W)).reshape(-1)
    col = jnp.broadcast_to(jnp.arange(GRID_W, dtype=F32)[None, :], (rows, GRID_W)).reshape(-1)
    inv_freq = ROPE_THETA ** (-2.0 * jnp.arange(AXIS_PAIRS, dtype=F32) / AXIS_ROPE)
    ang = jnp.concatenate([row[:, None] * inv_freq, col[:, None] * inv_freq], axis=1)
    cs, sn = jnp.cos(ang), jnp.sin(ang)
    one = jnp.ones((n_lat, QK_NOPE), F32)
    zn = jnp.zeros((n_lat, QK_NOPE), F32)
    zp = jnp.zeros((n_lat, HEAD_W - QK_DIM), F32)
    za = jnp.zeros((n_lat, AXIS_ROPE), F32)
    cos = jnp.concatenate([one, cs, cs, zp], axis=1)
    s_up = jnp.concatenate([zn, za, sn, zp], axis=1)
    s_dn = jnp.concatenate([zn, -sn, za, zp], axis=1)
    pad = lambda t, fill: jnp.concatenate([t, jnp.full((n_ctx, HEAD_W), fill, F32)], axis=0)
    return pad(cos, 1.0), pad(s_up, 0.0), pad(s_dn, 0.0)


def kernel(x, c, ctx, c_ctx, w_ada, b_ada, norm_mix, norm_ffn, w_in, conv_w, conv_b, conv_norm_g, conv_norm_b, q_a_norm, w_q_up, kv_a_norm, w_kv_up, q_head_norm, k_head_norm, w_out, router_w, router_b, w_gate_up, b_gate_up, w_down, b_down):
    bsz, n_lat, d = x.shape
    n_ctx = ctx.shape[1]
    assert bsz == 1 and w_ada.shape[0] == 1
    n_e = router_w.shape[-1]
    perm = _ROPE_PERM

    mod = _ada(jnp.stack([c[0], c_ctx], axis=1), w_ada[0], b_ada[0])
    mod = mod.reshape(2, N_MOD, d)
    mod1 = mod[:, 0:2, :]
    mod2 = jnp.stack([mod[0, 2], mod[0, 3], mod[0, 4]], axis=0)
    g2 = mod[0, 5][None]

    wi = w_in[0]
    kr_cols = jnp.concatenate([jnp.zeros((d, QK_NOPE), F32), wi[:, KR0:][:, jnp.asarray(perm)],
                               jnp.zeros((d, HEAD_W - QK_DIM), F32)], axis=1)
    win = jnp.concatenate([wi[:, :KR0], kr_cols], axis=1).astype(BF16)
    wq = _head_layout(_head_cols(w_q_up[0], QK_DIM, 0, QK_NOPE),
                      _head_cols(w_q_up[0], QK_DIM, QK_NOPE, QK_ROPE, perm)).astype(BF16)
    wk = _head_layout(_head_cols(w_kv_up[0], QK_NOPE + V_DIM, 0, QK_NOPE), None)
    wv = _head_layout(_head_cols(w_kv_up[0], QK_NOPE + V_DIM, QK_NOPE, V_DIM), None)
    wkv = jnp.concatenate([wk, wv], axis=1).astype(BF16)
    gq = _gain_row(q_head_norm[0], SCALE * math.log2(math.e))
    gk = _gain_row(k_head_norm[0])
    vone = jnp.zeros((N_HEADS, HEAD_W), F32).at[:, V_DIM].set(1.0).reshape(1, N_HEADS * HEAD_W)
    cos, s_up, s_dn = _rope_tables(n_lat, n_ctx)

    bound = (QK_DIM * SCALE * math.log2(math.e) * 1.01) * (jnp.max(jnp.abs(q_head_norm[0]))
                                                          * jnp.max(jnp.abs(k_head_norm[0])))
    shift = jnp.ceil(bound * 4.0) * 0.25
    fixed_ok = shift <= SHIFT_LIMIT
    lane = jnp.arange(HEAD_W)[None]
    qadd = jnp.where(lane == QK_DIM, jnp.where(fixed_ok, -shift, 0.0), 0.0).astype(F32)
    kadd = jnp.where(lane == QK_DIM, 1.0, 0.0).astype(F32)

    xin = jnp.concatenate([x[0], ctx[0]], axis=0)
    glu, q, k, vt = _inproj(xin, mod1, norm_mix, win, q_a_norm, wq, kv_a_norm, wkv, gq, gk, qadd, kadd, vone,
                            cos, s_up, s_dn, n_lat, tm=256)
    conv = _conv(glu, conv_w[0], conv_b, conv_norm_g, conv_norm_b, n_lat, tc=512)
    t_all = n_lat + n_ctx
    tk = max(t for t in range(256, 1025, 256) if t_all % t == 0)
    tq = min(1024, n_lat)
    attn = lax.cond(fixed_ok,
                    lambda: _attn(_attn_fixed_kernel, q, k, vt, n_lat, tq, tk),
                    lambda: _attn(_attn_online_kernel, q, k, vt, n_lat, tq, tk))

    rw = jnp.concatenate([router_w[0], jnp.zeros((d, LANES - n_e), F32)], axis=1)
    rb = jnp.concatenate([router_b[0], jnp.full((LANES - n_e,), -1e30, F32)])[None]
    x1, h2, meta, gates, cnt = _outproj(conv, attn, x[0], w_out[0].astype(BF16), mod2, norm_ffn, rw, rb,
                                        tm=256)

    counts = cnt[0, :n_e].astype(jnp.int32)
    top_idx = meta[:, :TOP_K]
    pos = meta[:, TOP_K:2 * TOP_K]
    nblk_e = (counts + ROW_BLK - 1) // ROW_BLK
    blk_end = jnp.cumsum(nblk_e)
    pad_start = (blk_end - nblk_e) * ROW_BLK
    dest = pad_start[top_idx] + pos
    n_blocks = (n_lat * TOP_K) // ROW_BLK + n_e
    n_rows = n_blocks * ROW_BLK
    tok = jnp.broadcast_to(jnp.arange(n_lat, dtype=jnp.int32)[:, None], (n_lat, TOP_K))
    row_tok = jnp.zeros((n_rows,), jnp.int32).at[dest.reshape(-1)].set(tok.reshape(-1))
    block_e = jnp.minimum(jnp.sum(blk_end[None, :] <= jnp.arange(n_blocks)[:, None], axis=1),
                          n_e - 1).astype(jnp.int32)
    n_used = blk_end[-1:].astype(jnp.int32)

    xb = _gather_rows(row_tok, h2, nb=256)
    yb = _experts(block_e, n_used, xb, w_gate_up[0], b_gate_up[0], w_down[0], b_down[0], tf=512)

    tm_c = 128
    dest_km = dest.reshape(n_lat // tm_c, tm_c, TOP_K).transpose(0, 2, 1).reshape(n_lat // tm_c, 1,
                                                                                  TOP_K * tm_c)
    out = _combine(dest_km, yb, x1, gates, g2, tm=tm_c)
    return out[None]
```

```python
import functools
import math

import jax
import jax.numpy as jnp
from jax import lax
from jax.experimental import pallas as pl
from jax.experimental.pallas import tpu as pltpu

F32 = jnp.float32
BF16 = jnp.bfloat16

EPS = 1e-6
N_MOD = 6
GRID_W = 64
CONV_CH = 1024
CONV_GROUPS = 8
CONV_WIDTH = 31
N_HEADS = 16
Q_LORA = 512
KV_LORA = 256
QK_NOPE = 64
QK_ROPE = 32
V_DIM = 64
QK_DIM = QK_NOPE + QK_ROPE
SCALE = QK_DIM ** -0.5
ROPE_THETA = 10000.0
AXIS_ROPE = QK_ROPE // 2
AXIS_PAIRS = AXIS_ROPE // 2
Q0 = 2 * CONV_CH
KV0 = Q0 + Q_LORA
KR0 = KV0 + KV_LORA
N_EXPERTS = 32
TOP_K = 4
SWIGLU_LIMIT = 7.0
SWIGLU_ALPHA = 1.702

LANES = 128
HEAD_W = LANES
ROW_BLK = 128
VMEM_LIMIT = 56 * 1024 * 1024

_ROPE_PERM = tuple(ax * AXIS_ROPE + ab * AXIS_PAIRS + p
                   for ab in range(2) for ax in range(2) for p in range(AXIS_PAIRS))


def _cparams(sem, limit=VMEM_LIMIT):
    return pltpu.CompilerParams(dimension_semantics=sem, vmem_limit_bytes=limit)


def _const_spec(shape):
    nd = len(shape)
    return pl.BlockSpec(shape, lambda *_: (0,) * nd)


def _ada_kernel(c_ref, w_ref, b_ref, o_ref, *, rc):
    d, tn = w_ref.shape

    def body(k, carry):
        a0, a1 = carry
        r0 = pl.multiple_of(k * rc, rc)
        cc = c_ref[pl.ds(r0, rc), :]
        s = cc * jax.nn.sigmoid(cc)
        w = w_ref[pl.ds(r0, rc), :]
        p0 = (w * s[:, 0:1]).reshape(rc // 8, 8, tn).sum(axis=0)
        p1 = (w * s[:, 1:2]).reshape(rc // 8, 8, tn).sum(axis=0)
        return a0 + p0, a1 + p1

    z = jnp.zeros((8, tn), F32)
    a0, a1 = lax.fori_loop(0, d // rc, body, (z, z))
    b = b_ref[...]
    o_ref[0:1, :] = jnp.sum(a0, axis=0, keepdims=True) + b
    o_ref[1:2, :] = jnp.sum(a1, axis=0, keepdims=True) + b


def _ada(c2, w_ada, b_ada):
    d, n = w_ada.shape
    tn = 1024
    return pl.pallas_call(
        functools.partial(_ada_kernel, rc=128),
        out_shape=jax.ShapeDtypeStruct((2, n), F32),
        grid=(n // tn,),
        in_specs=[_const_spec((d, 2)),
                  pl.BlockSpec((d, tn), lambda j: (0, j)),
                  pl.BlockSpec((1, tn), lambda j: (0, j))],
        out_specs=pl.BlockSpec((2, tn), lambda j: (0, j)),
        compiler_params=_cparams(("arbitrary",)),
        name="ada",
    )(c2, w_ada, b_ada.reshape(1, n))


def _head_norm_rope(t, gain, cos, s_up, s_dn):
    r = lax.rsqrt(jnp.sum(t * t, axis=-1, keepdims=True) * (1.0 / QK_DIM) + EPS)
    t = t * r * gain
    return t * cos + pltpu.roll(t, AXIS_ROPE, 1) * s_up + pltpu.roll(t, HEAD_W - AXIS_ROPE, 1) * s_dn


def _inproj_kernel(x_ref, mod_ref, g_ref, win_ref, qa_ref, wq_ref, kva_ref, wkv_ref, gq_ref, gk_ref,
                   qadd_ref, kadd_ref, vone_ref, cos_ref, sup_ref, sdn_ref, glu_ref, q_ref, k_ref, vt_ref):
    x = x_ref[...]
    ms = jnp.mean(x * x, axis=-1, keepdims=True)
    y = x * lax.rsqrt(ms + EPS) * g_ref[...]
    h = y * (1.0 + mod_ref[0, 1:2, :]) + mod_ref[0, 0:1, :]
    proj = jnp.dot(h.astype(BF16), win_ref[...], preferred_element_type=F32)
    glu_ref[...] = proj[:, :CONV_CH] * jax.nn.sigmoid(proj[:, CONV_CH:Q0])

    pq = proj[:, Q0:KV0]
    pq = pq * lax.rsqrt(jnp.mean(pq * pq, axis=-1, keepdims=True) + EPS) * qa_ref[...]
    q = jnp.dot(pq.astype(BF16), wq_ref[...], preferred_element_type=F32)
    pkv = proj[:, KV0:KR0]
    pkv = pkv * lax.rsqrt(jnp.mean(pkv * pkv, axis=-1, keepdims=True) + EPS) * kva_ref[...]
    kv = jnp.dot(pkv.astype(BF16), wkv_ref[...], preferred_element_type=F32)
    kr = proj[:, KR0:KR0 + HEAD_W]

    cos, s_up, s_dn = cos_ref[...], sup_ref[...], sdn_ref[...]
    gq, gk = gq_ref[...], gk_ref[...]
    qadd, kadd = qadd_ref[...], kadd_ref[...]
    for hd in range(N_HEADS):
        sl = slice(hd * HEAD_W, (hd + 1) * HEAD_W)
        q_ref[:, sl] = (_head_norm_rope(q[:, sl], gq, cos, s_up, s_dn) + qadd).astype(BF16)
        k_ref[:, sl] = (_head_norm_rope(kv[:, sl] + kr, gk, cos, s_up, s_dn) + kadd).astype(BF16)
    vt_ref[...] = (kv[:, N_HEADS * HEAD_W:] + vone_ref[...]).T.astype(BF16)


def _inproj(xin, mod1, g_mix, win, qa, wq, kva, wkv, gq, gk, qadd, kadd, vone, cos, s_up, s_dn, n_lat, tm):
    t_all, d = xin.shape
    n_lat_tiles = n_lat // tm
    hw = N_HEADS * HEAD_W
    row = lambda w: pl.BlockSpec((tm, w), lambda i: (i, 0))
    return pl.pallas_call(
        _inproj_kernel,
        out_shape=(jax.ShapeDtypeStruct((t_all, CONV_CH), F32),
                   jax.ShapeDtypeStruct((t_all, hw), BF16),
                   jax.ShapeDtypeStruct((t_all, hw), BF16),
                   jax.ShapeDtypeStruct((hw, t_all), BF16)),
        grid=(t_all // tm,),
        in_specs=[row(d),
                  pl.BlockSpec((1, 2, d), lambda i: (jnp.where(i < n_lat_tiles, 0, 1), 0, 0)),
                  _const_spec(g_mix.shape), _const_spec(win.shape), _const_spec(qa.shape),
                  _const_spec(wq.shape), _const_spec(kva.shape), _const_spec(wkv.shape),
                  _const_spec(gq.shape), _const_spec(gk.shape), _const_spec(qadd.shape),
                  _const_spec(kadd.shape), _const_spec(vone.shape),
                  row(HEAD_W), row(HEAD_W), row(HEAD_W)],
        out_specs=(row(CONV_CH), row(hw), row(hw), pl.BlockSpec((hw, tm), lambda i: (0, i))),
        compiler_params=_cparams(("arbitrary",)),
        name="inproj",
    )(xin, mod1, g_mix, win, qa, wq, kva, wkv, gq, gk, qadd, kadd, vone, cos, s_up, s_dn)


CONV_HALO = 16
CONV_RC = 64


def _conv_kernel(prev_ref, main_ref, next_ref, w_ref, b_ref, g_ref, beta_ref, o_ref, buf_ref):
    i = pl.program_id(0)
    tc = main_ref.shape[0]
    gw = CONV_CH // CONV_GROUPS
    buf_ref[0:CONV_HALO, :] = jnp.where(i > 0, prev_ref[...], 0.0)
    buf_ref[CONV_HALO:CONV_HALO + tc, :] = main_ref[...]
    buf_ref[CONV_HALO + tc:, :] = jnp.where(i < pl.num_programs(0) - 1, next_ref[...], 0.0)
    win = CONV_RC + 2 * CONV_HALO

    def chunk(c, carry):
        r0 = pl.multiple_of(c * CONV_RC, CONV_RC)
        for g in range(CONV_GROUPS):
            ls = slice(g * gw, (g + 1) * gw)
            w_all = buf_ref[pl.ds(r0, win), ls]
            acc = jnp.zeros((CONV_RC, gw), F32)
            for b in range(8):
                wb = w_all[b:b + CONV_RC + 24, :]
                for a in range(4):
                    k = 8 * a + b - 1
                    if 0 <= k < CONV_WIDTH:
                        acc = acc + wb[8 * a:8 * a + CONV_RC, :] * w_ref[k:k + 1, ls]
            acc = acc + b_ref[:, ls]
            mu = jnp.mean(acc, axis=-1, keepdims=True)
            cen = acc - mu
            var = jnp.mean(cen * cen, axis=-1, keepdims=True)
            yv = cen * lax.rsqrt(var + EPS) * g_ref[:, ls] + beta_ref[:, ls]
            o_ref[pl.ds(r0, CONV_RC), ls] = (yv * jax.nn.sigmoid(yv)).astype(o_ref.dtype)
        return carry

    lax.fori_loop(0, tc // CONV_RC, chunk, 0)


def _conv(glu, w, b, g, beta, n_lat, tc):
    hb = tc // CONV_HALO
    return pl.pallas_call(
        _conv_kernel,
        out_shape=jax.ShapeDtypeStruct((n_lat, CONV_CH), BF16),
        grid=(n_lat // tc,),
        in_specs=[pl.BlockSpec((CONV_HALO, CONV_CH), lambda i: (jnp.maximum(i * hb - 1, 0), 0)),
                  pl.BlockSpec((tc, CONV_CH), lambda i: (i, 0)),
                  pl.BlockSpec((CONV_HALO, CONV_CH), lambda i: ((i + 1) * hb, 0)),
                  _const_spec(w.shape), _const_spec(b.shape), _const_spec(g.shape),
                  _const_spec(beta.shape)],
        out_specs=pl.BlockSpec((tc, CONV_CH), lambda i: (i, 0)),
        scratch_shapes=[pltpu.VMEM((tc + 2 * CONV_HALO, CONV_CH), F32)],
        compiler_params=_cparams(("arbitrary",)),
        name="conv",
    )(glu, glu, glu, w, b, g, beta)


ATTN_HEADS_PER_STEP = 2
SHIFT_LIMIT = 40.0


def _attn_fixed_kernel(q_ref, k_ref, vt_ref, o_ref, *, tk):
    tq = q_ref.shape[0]
    n_kv = k_ref.shape[0] // tk
    qs = [q_ref[:, hh * HEAD_W:(hh + 1) * HEAD_W] for hh in range(ATTN_HEADS_PER_STEP)]

    def step(j, carry):
        r0 = pl.multiple_of(j * tk, tk)
        out = []
        for hh in range(ATTN_HEADS_PER_STEP):
            kc = k_ref[pl.ds(r0, tk), hh * HEAD_W:(hh + 1) * HEAD_W]
            vtc = vt_ref[hh * HEAD_W:(hh + 1) * HEAD_W, pl.ds(r0, tk)]
            st = lax.dot_general(kc, qs[hh], (((1,), (1,)), ((), ())), preferred_element_type=F32)
            pt = jnp.exp2(st).astype(BF16)
            out.append(carry[hh] + jnp.dot(vtc, pt, preferred_element_type=F32))
        return tuple(out)

    init = tuple(jnp.zeros((HEAD_W, tq), F32) for _ in range(ATTN_HEADS_PER_STEP))
    res = lax.fori_loop(0, n_kv, step, init)
    outs = [(acc[:V_DIM, :] / acc[V_DIM:V_DIM + 1, :]).T for acc in res]
    o_ref[...] = jnp.concatenate(outs, axis=-1).astype(o_ref.dtype)


def _attn_online_kernel(q_ref, k_ref, vt_ref, o_ref, *, tk):
    tq = q_ref.shape[0]
    n_kv = k_ref.shape[0] // tk
    qs = [q_ref[:, hh * HEAD_W:(hh + 1) * HEAD_W] for hh in range(ATTN_HEADS_PER_STEP)]

    def step(j, carry):
        r0 = pl.multiple_of(j * tk, tk)
        out = []
        for hh in range(ATTN_HEADS_PER_STEP):
            m, acc = carry[hh]
            kc = k_ref[pl.ds(r0, tk), hh * HEAD_W:(hh + 1) * HEAD_W]
            vtc = vt_ref[hh * HEAD_W:(hh + 1) * HEAD_W, pl.ds(r0, tk)]
            st = lax.dot_general(kc, qs[hh], (((1,), (1,)), ((), ())), preferred_element_type=F32)
            m_new = jnp.maximum(m, jnp.max(st, axis=0, keepdims=True))
            pt = jnp.exp2(st - m_new).astype(BF16)
            acc = acc * jnp.exp2(m - m_new) + jnp.dot(vtc, pt, preferred_element_type=F32)
            out.append((m_new, acc))
        return tuple(out)

    init = tuple((jnp.full((1, tq), -jnp.inf, F32), jnp.zeros((HEAD_W, tq), F32))
                 for _ in range(ATTN_HEADS_PER_STEP))
    res = lax.fori_loop(0, n_kv, step, init)
    outs = [(acc[:V_DIM, :] / acc[V_DIM:V_DIM + 1, :]).T for _, acc in res]
    o_ref[...] = jnp.concatenate(outs, axis=-1).astype(o_ref.dtype)


def _attn(body, q, k, vt, n_lat, tq, tk):
    t_all = k.shape[0]
    pw = ATTN_HEADS_PER_STEP * HEAD_W
    return pl.pallas_call(
        functools.partial(body, tk=tk),
        out_shape=jax.ShapeDtypeStruct((n_lat, N_HEADS * V_DIM), BF16),
        grid=(N_HEADS // ATTN_HEADS_PER_STEP, n_lat // tq),
        in_specs=[pl.BlockSpec((tq, pw), lambda hp, i: (i, hp)),
                  pl.BlockSpec((t_all, pw), lambda hp, i: (0, hp)),
                  pl.BlockSpec((pw, t_all), lambda hp, i: (hp, 0))],
        out_specs=pl.BlockSpec((tq, ATTN_HEADS_PER_STEP * V_DIM), lambda hp, i: (i, hp)),
        compiler_params=_cparams(("arbitrary", "arbitrary")),
        name="attn",
    )(q, k, vt)


def _outproj_kernel(conv_ref, attn_ref, x_ref, wo_ref, mod_ref, g_ref, rw_ref, rb_ref,
                    x1_ref, h2_ref, meta_ref, gate_ref, cnt_ref, cnt_acc):
    i = pl.program_id(0)
    tm = x_ref.shape[0]

    @pl.when(i == 0)
    def _():
        cnt_acc[...] = jnp.zeros_like(cnt_acc)

    y = jnp.dot(conv_ref[...], wo_ref[:CONV_CH, :], preferred_element_type=F32)
    y = y + jnp.dot(attn_ref[...], wo_ref[CONV_CH:, :], preferred_element_type=F32)
    x1 = x_ref[...] + mod_ref[0:1, :] * y
    x1_ref[...] = x1
    ms = jnp.mean(x1 * x1, axis=-1, keepdims=True)
    h2 = x1 * lax.rsqrt(ms + EPS) * g_ref[...]
    h2 = h2 * (1.0 + mod_ref[2:3, :]) + mod_ref[1:2, :]
    h2_ref[...] = h2

    h_hi = h2.astype(BF16)
    h_lo = (h2 - h_hi.astype(F32)).astype(BF16)
    rw = rw_ref[...]
    r_hi = rw.astype(BF16)
    r_lo = (rw - r_hi.astype(F32)).astype(BF16)
    logits = (jnp.dot(h_hi, r_hi, preferred_element_type=F32)
              + jnp.dot(h_lo, r_hi, preferred_element_type=F32)
              + jnp.dot(h_hi, r_lo, preferred_element_type=F32)) + rb_ref[...]

    lane = lax.broadcasted_iota(jnp.int32, (tm, LANES), 1)
    work = logits
    vals, idxs, sels = [], [], []
    for _ in range(TOP_K):
        mx = jnp.max(work, axis=-1, keepdims=True)
        idx = jnp.min(jnp.where(work == mx, lane, LANES), axis=-1, keepdims=True)
        sel = lane == idx
        vals.append(mx)
        idxs.append(idx)
        sels.append(sel)
        work = jnp.where(sel, -jnp.inf, work)
    es = [jnp.exp(v - vals[0]) for v in vals]
    denom = es[0] + es[1] + es[2] + es[3]

    chosen = jnp.zeros((tm, LANES), F32)
    for sel in sels:
        chosen = chosen + jnp.where(sel, 1.0, 0.0)
    rr = lax.broadcasted_iota(jnp.int32, (tm, tm), 0)
    cc = lax.broadcasted_iota(jnp.int32, (tm, tm), 1)
    tri = jnp.where(cc < rr, 1.0, 0.0).astype(BF16)
    before = jnp.dot(tri, chosen.astype(BF16), preferred_element_type=F32) + cnt_acc[...]

    meta = jnp.zeros((tm, LANES), jnp.int32)
    gates = jnp.zeros((tm, LANES), F32)
    for kk in range(TOP_K):
        pos = jnp.sum(jnp.where(sels[kk], before, 0.0), axis=-1, keepdims=True).astype(jnp.int32)
        meta = jnp.where(lane == kk, idxs[kk], meta)
        meta = jnp.where(lane == TOP_K + kk, pos, meta)
        gates = jnp.where(lane == kk, es[kk] / denom, gates)
    meta_ref[...] = meta
    gate_ref[...] = gates
    cnt_acc[...] = cnt_acc[...] + jnp.sum(chosen, axis=0, keepdims=True)
    cnt_ref[...] = cnt_acc[...]


def _outproj(conv, attn, x, wo, mod2, g_ffn, rw, rb, tm):
    n_lat, d = x.shape
    row = lambda w: pl.BlockSpec((tm, w), lambda i: (i, 0))
    return pl.pallas_call(
        _outproj_kernel,
        out_shape=(jax.ShapeDtypeStruct((n_lat, d), F32),
                   jax.ShapeDtypeStruct((n_lat, d), F32),
                   jax.ShapeDtypeStruct((n_lat, LANES), jnp.int32),
                   jax.ShapeDtypeStruct((n_lat, LANES), F32),
                   jax.ShapeDtypeStruct((1, LANES), F32)),
        grid=(n_lat // tm,),
        in_specs=[row(CONV_CH), row(N_HEADS * V_DIM), row(d), _const_spec(wo.shape),
                  _const_spec(mod2.shape), _const_spec(g_ffn.shape), _const_spec(rw.shape),
                  _const_spec(rb.shape)],
        out_specs=(row(d), row(d), row(LANES), row(LANES), _const_spec((1, LANES))),
        scratch_shapes=[pltpu.VMEM((1, LANES), F32)],
        compiler_params=_cparams(("arbitrary",)),
        name="outproj",
    )(conv, attn, x, wo, mod2, g_ffn, rw, rb)


def _gather_kernel(tok_ref, h_hbm, o_ref, buf, sem):
    nb = buf.shape[0]

    def row_copy(r):
        return pltpu.make_async_copy(h_hbm.at[pl.ds(tok_ref[0, 0, r], 1)], buf.at[pl.ds(r, 1)], sem)

    def issue(r, c):
        row_copy(r).start()
        return c

    lax.fori_loop(0, nb, issue, 0)

    def drain(r, c):
        row_copy(r).wait()
        return c

    lax.fori_loop(0, nb, drain, 0)
    o_ref[...] = buf[...].astype(o_ref.dtype)


def _gather_rows(row_tok, h2, nb):
    n_rows = row_tok.shape[0]
    d = h2.shape[1]
    return pl.pallas_call(
        _gather_kernel,
        out_shape=jax.ShapeDtypeStruct((n_rows, d), BF16),
        grid=(n_rows // nb,),
        in_specs=[pl.BlockSpec((1, 1, nb), lambda i: (i, 0, 0), memory_space=pltpu.SMEM),
                  pl.BlockSpec(memory_space=pl.ANY)],
        out_specs=pl.BlockSpec((nb, d), lambda i: (i, 0)),
        scratch_shapes=[pltpu.VMEM((nb, d), F32), pltpu.SemaphoreType.DMA],
        compiler_params=_cparams(("arbitrary",)),
        name="gather",
    )(row_tok.reshape(n_rows // nb, 1, nb), h2)


SB_BLOCKS = 12
EXPERT_TF = 256
EXPERT_GROUP = 4


def _expert_kernel(sbe_ref, sbb0_ref, sbnb_ref, nsup_ref, nused_ref,
                   xs_hbm, wg_ref, wl_ref, wd_ref, bg_ref, bl_ref, bd_ref, ys_hbm,
                   xbuf, acc, wgl_s, wd_s, zbuf, xsem, ysem):
    s = pl.program_id(0)
    f = pl.program_id(1)
    nf = pl.num_programs(1)
    tf = wd_s.shape[0]
    n_sup = nsup_ref[0]
    n_blocks = ys_hbm.shape[0] // ROW_BLK

    def x_copy(sb, slot, b):
        row0 = pl.multiple_of((sbb0_ref[sb] + b) * ROW_BLK, ROW_BLK)
        return pltpu.make_async_copy(xs_hbm.at[pl.ds(row0, ROW_BLK)],
                                     xbuf.at[slot, pl.ds(pl.multiple_of(b * ROW_BLK, ROW_BLK), ROW_BLK)],
                                     xsem.at[slot])

    def y_copy(sb, b):
        row0 = pl.multiple_of((sbb0_ref[sb] + b) * ROW_BLK, ROW_BLK)
        return pltpu.make_async_copy(acc.at[pl.ds(pl.multiple_of(b * ROW_BLK, ROW_BLK), ROW_BLK)],
                                     ys_hbm.at[pl.ds(row0, ROW_BLK)], ysem)

    def z_copy(b):
        return pltpu.make_async_copy(zbuf, ys_hbm.at[pl.ds(pl.multiple_of(b * ROW_BLK, ROW_BLK), ROW_BLK)],
                                     ysem)

    def for_blocks(n, fn):
        def body(b, c):
            fn(b)
            return c
        lax.fori_loop(0, n, body, 0)

    @pl.when(s < n_sup)
    def _():
        slot = lax.rem(s, 2)
        nb = sbnb_ref[s]

        @pl.when(f == 0)
        def _():
            @pl.when(s == 0)
            def _():
                for_blocks(nb, lambda b: x_copy(0, 0, b).start())

            for_blocks(nb, lambda b: x_copy(s, slot, b).wait())

            @pl.when(s + 1 < n_sup)
            def _():
                nxt = jnp.minimum(s + 1, n_sup - 1)
                for_blocks(sbnb_ref[nxt], lambda b: x_copy(nxt, 1 - slot, b).start())

        wgl_s[:, :tf] = wg_ref[0].astype(BF16)
        wgl_s[:, tf:] = wl_ref[0].astype(BF16)
        wd_s[...] = wd_ref[0].astype(BF16)
        bg, bl = bg_ref[0], bl_ref[0]

        @pl.when(f == 0)
        def _():
            bias = jnp.broadcast_to(bd_ref[0], (ROW_BLK, acc.shape[1]))

            def init(b):
                acc[pl.ds(pl.multiple_of(b * ROW_BLK, ROW_BLK), ROW_BLK), :] = bias
            for_blocks(nb, init)

        def add_tile(b, m):
            rows = m * ROW_BLK
            r0 = pl.multiple_of(b * ROW_BLK, ROW_BLK)
            gu = jnp.dot(xbuf[slot, pl.ds(r0, rows), :], wgl_s[...], preferred_element_type=F32)
            glu = jnp.minimum(gu[:, :tf] + bg, SWIGLU_LIMIT)
            lin = jnp.clip(gu[:, tf:] + bl, -SWIGLU_LIMIT, SWIGLU_LIMIT)
            hmid = glu * jax.nn.sigmoid(SWIGLU_ALPHA * glu) * (lin + 1.0)
            acc[pl.ds(r0, rows), :] += jnp.dot(hmid.astype(BF16), wd_s[...], preferred_element_type=F32)

            @pl.when(f == nf - 1)
            def _():
                for u in range(m):
                    y_copy(s, b + u).start()

        n_grp = nb // EXPERT_GROUP
        for_blocks(n_grp, lambda g: add_tile(g * EXPERT_GROUP, EXPERT_GROUP))
        done = n_grp * EXPERT_GROUP
        m = EXPERT_GROUP // 2
        while m >= 1:
            take = ((nb - done) // m) > 0

            @pl.when(take)
            def _(m=m, done=done):
                add_tile(done, m)
            done = done + jnp.where(take, m, 0)
            m //= 2

        @pl.when(f == nf - 1)
        def _():
            for_blocks(nb, lambda b: y_copy(s, b).wait())

    @pl.when((s == pl.num_programs(0) - 1) & (f == nf - 1))
    def _():
        zbuf[...] = jnp.zeros_like(zbuf)
        n_used = nused_ref[0]

        def tail(fn):
            def body(b, c):
                fn(b)
                return c
            lax.fori_loop(n_used, n_blocks, body, 0)

        tail(lambda b: z_copy(b).start())
        tail(lambda b: z_copy(b).wait())


def _experts(sb_e, sb_b0, sb_nb, n_sup, n_used, xs, w_gate_up, b_gate_up, w_down, b_down):
    n_rows, d = xs.shape
    tf = EXPERT_TF
    de = w_down.shape[1]
    nf = de // tf
    n_e = w_down.shape[0]
    n_sb = sb_e.shape[0]
    rs = SB_BLOCKS * ROW_BLK
    bgu = b_gate_up.reshape(n_e, 1, 2 * de)
    bdn = b_down.reshape(n_e, 1, d)

    def e_of(s, nsup, sbe):
        return sbe[jnp.minimum(s, nsup[0] - 1)]

    def f_of(s, f, nsup):
        return jnp.where(s < nsup[0], f, nf - 1)

    grid_spec = pltpu.PrefetchScalarGridSpec(
        num_scalar_prefetch=5,
        grid=(n_sb, nf),
        in_specs=[
            pl.BlockSpec(memory_space=pl.ANY),
            pl.BlockSpec((1, d, tf), lambda s, f, sbe, b0, nb, ns, nu: (e_of(s, ns, sbe), 0, f_of(s, f, ns))),
            pl.BlockSpec((1, d, tf), lambda s, f, sbe, b0, nb, ns, nu: (e_of(s, ns, sbe), 0, nf + f_of(s, f, ns))),
            pl.BlockSpec((1, tf, d), lambda s, f, sbe, b0, nb, ns, nu: (e_of(s, ns, sbe), f_of(s, f, ns), 0)),
            pl.BlockSpec((1, 1, tf), lambda s, f, sbe, b0, nb, ns, nu: (e_of(s, ns, sbe), 0, f_of(s, f, ns))),
            pl.BlockSpec((1, 1, tf), lambda s, f, sbe, b0, nb, ns, nu: (e_of(s, ns, sbe), 0, nf + f_of(s, f, ns))),
            pl.BlockSpec((1, 1, d), lambda s, f, sbe, b0, nb, ns, nu: (e_of(s, ns, sbe), 0, 0)),
        ],
        out_specs=pl.BlockSpec(memory_space=pl.ANY),
        scratch_shapes=[pltpu.VMEM((2, rs, d), BF16), pltpu.VMEM((rs, d), F32),
                        pltpu.VMEM((d, 2 * tf), BF16), pltpu.VMEM((tf, d), BF16),
                        pltpu.VMEM((ROW_BLK, d), F32),
                        pltpu.SemaphoreType.DMA((2,)), pltpu.SemaphoreType.DMA],
    )
    return pl.pallas_call(
        _expert_kernel,
        out_shape=jax.ShapeDtypeStruct((n_rows, d), F32),
        grid_spec=grid_spec,
        compiler_params=_cparams(("arbitrary", "arbitrary")),
        name="experts",
    )(sb_e, sb_b0, sb_nb, n_sup, n_used, xs, w_gate_up, w_gate_up, w_down, bgu, bgu, bdn)


def _combine_kernel(dest_ref, y_hbm, x1_ref, gate_ref, g2_ref, o_ref, buf, sem):
    tm = x1_ref.shape[0]
    n = TOP_K * tm

    def row_copy(r):
        return pltpu.make_async_copy(y_hbm.at[pl.ds(dest_ref[0, 0, r], 1)], buf.at[pl.ds(r, 1)], sem)

    def issue(r, c):
        row_copy(r).start()
        return c

    lax.fori_loop(0, n, issue, 0)

    def drain(r, c):
        row_copy(r).wait()
        return c

    lax.fori_loop(0, n, drain, 0)
    gates = gate_ref[...]
    acc = buf[0:tm, :] * gates[:, 0:1]
    for kk in range(1, TOP_K):
        acc = acc + buf[kk * tm:(kk + 1) * tm, :] * gates[:, kk:kk + 1]
    o_ref[...] = x1_ref[...] + g2_ref[...] * acc


def _combine(dest_km, yb, x1, gates, g2, tm):
    n_lat, d = x1.shape
    return pl.pallas_call(
        _combine_kernel,
        out_shape=jax.ShapeDtypeStruct((n_lat, d), F32),
        grid=(n_lat // tm,),
        in_specs=[pl.BlockSpec((1, 1, TOP_K * tm), lambda i: (i, 0, 0), memory_space=pltpu.SMEM),
                  pl.BlockSpec(memory_space=pl.ANY),
                  pl.BlockSpec((tm, d), lambda i: (i, 0)),
                  pl.BlockSpec((tm, LANES), lambda i: (i, 0)),
                  _const_spec((1, d))],
        out_specs=pl.BlockSpec((tm, d), lambda i: (i, 0)),
        scratch_shapes=[pltpu.VMEM((TOP_K * tm, d), F32), pltpu.SemaphoreType.DMA],
        compiler_params=_cparams(("arbitrary",)),
        name="combine",
    )(dest_km, yb, x1, gates, g2)


def _head_cols(w, per_head, start, width, perm=None):
    k = w.shape[0]
    t = w.reshape(k, N_HEADS, per_head)[:, :, start:start + width]
    return t if perm is None else t[:, :, jnp.asarray(perm)]


def _head_layout(nope, rope):
    k = nope.shape[0]
    parts = [nope]
    if rope is not None:
        parts.append(rope)
    used = sum(p.shape[-1] for p in parts)
    parts.append(jnp.zeros((k, N_HEADS, HEAD_W - used), nope.dtype))
    return jnp.concatenate(parts, axis=-1).reshape(k, N_HEADS * HEAD_W)


def _gain_row(g, mult=1.0):
    perm = jnp.asarray(_ROPE_PERM)
    return jnp.concatenate([g[:QK_NOPE], g[QK_NOPE:][perm], jnp.zeros((HEAD_W - QK_DIM,), F32)])[None] * mult


def _rope_tables(n_lat, n_ctx):
    rows = n_lat // GRID_W
    row = jnp.broadcast_to(jnp.arange(rows, dtype=F32)[:, None], (rows, GRID_W)).reshape(-1)
    col = jnp.broadcast_to(jnp.arange(GRID_W, dtype=F32)[None, :], (rows, GRID_W)).reshape(-1)
    inv_freq = ROPE_THETA ** (-2.0 * jnp.arange(AXIS_PAIRS, dtype=F32) / AXIS_ROPE)
    ang = jnp.concatenate([row[:, None] * inv_freq, col[:, None] * inv_freq], axis=1)
    cs, sn = jnp.cos(ang), jnp.sin(ang)
    one = jnp.ones((n_lat, QK_NOPE), F32)
    zn = jnp.zeros((n_lat, QK_NOPE), F32)
    zp = jnp.zeros((n_lat, HEAD_W - QK_DIM), F32)
    za = jnp.zeros((n_lat, AXIS_ROPE), F32)
    cos = jnp.concatenate([one, cs, cs, zp], axis=1)
    s_up = jnp.concatenate([zn, za, sn, zp], axis=1)
    s_dn = jnp.concatenate([zn, -sn, za, zp], axis=1)
    pad = lambda t, fill: jnp.concatenate([t, jnp.full((n_ctx, HEAD_W), fill, F32)], axis=0)
    return pad(cos, 1.0), pad(s_up, 0.0), pad(s_dn, 0.0)


def kernel(x, c, ctx, c_ctx, w_ada, b_ada, norm_mix, norm_ffn, w_in, conv_w, conv_b, conv_norm_g, conv_norm_b, q_a_norm, w_q_up, kv_a_norm, w_kv_up, q_head_norm, k_head_norm, w_out, router_w, router_b, w_gate_up, b_gate_up, w_down, b_down):
    bsz, n_lat, d = x.shape
    n_ctx = ctx.shape[1]
    assert bsz == 1 and w_ada.shape[0] == 1
    n_e = router_w.shape[-1]
    perm = _ROPE_PERM

    mod = _ada(jnp.stack([c[0], c_ctx], axis=1), w_ada[0], b_ada[0])
    mod = mod.reshape(2, N_MOD, d)
    mod1 = mod[:, 0:2, :]
    mod2 = jnp.stack([mod[0, 2], mod[0, 3], mod[0, 4]], axis=0)
    g2 = mod[0, 5][None]

    wi = w_in[0]
    kr_cols = jnp.concatenate([jnp.zeros((d, QK_NOPE), F32), wi[:, KR0:][:, jnp.asarray(perm)],
                               jnp.zeros((d, HEAD_W - QK_DIM), F32)], axis=1)
    win = jnp.concatenate([wi[:, :KR0], kr_cols], axis=1).astype(BF16)
    wq = _head_layout(_head_cols(w_q_up[0], QK_DIM, 0, QK_NOPE),
                      _head_cols(w_q_up[0], QK_DIM, QK_NOPE, QK_ROPE, perm)).astype(BF16)
    wk = _head_layout(_head_cols(w_kv_up[0], QK_NOPE + V_DIM, 0, QK_NOPE), None)
    wv = _head_layout(_head_cols(w_kv_up[0], QK_NOPE + V_DIM, QK_NOPE, V_DIM), None)
    wkv = jnp.concatenate([wk, wv], axis=1).astype(BF16)
    gq = _gain_row(q_head_norm[0], SCALE * math.log2(math.e))
    gk = _gain_row(k_head_norm[0])
    vone = jnp.zeros((N_HEADS, HEAD_W), F32).at[:, V_DIM].set(1.0).reshape(1, N_HEADS * HEAD_W)
    cos, s_up, s_dn = _rope_tables(n_lat, n_ctx)

    bound = (QK_DIM * SCALE * math.log2(math.e) * 1.01) * (jnp.max(jnp.abs(q_head_norm[0]))
                                                          * jnp.max(jnp.abs(k_head_norm[0])))
    shift = jnp.ceil(bound * 4.0) * 0.25
    fixed_ok = shift <= SHIFT_LIMIT
    lane = jnp.arange(HEAD_W)[None]
    qadd = jnp.where(lane == QK_DIM, jnp.where(fixed_ok, -shift, 0.0), 0.0).astype(F32)
    kadd = jnp.where(lane == QK_DIM, 1.0, 0.0).astype(F32)

    xin = jnp.concatenate([x[0], ctx[0]], axis=0)
    glu, q, k, vt = _inproj(xin, mod1, norm_mix, win, q_a_norm, wq, kv_a_norm, wkv, gq, gk, qadd, kadd, vone,
                            cos, s_up, s_dn, n_lat, tm=256)
    conv = _conv(glu, conv_w[0], conv_b, conv_norm_g, conv_norm_b, n_lat, tc=512)
    t_all = n_lat + n_ctx
    tk = max(t for t in range(256, 1025, 256) if t_all % t == 0)
    tq = min(1024, n_lat)
    attn = lax.cond(fixed_ok,
                    lambda: _attn(_attn_fixed_kernel, q, k, vt, n_lat, tq, tk),
                    lambda: _attn(_attn_online_kernel, q, k, vt, n_lat, tq, tk))

    rw = jnp.concatenate([router_w[0], jnp.zeros((d, LANES - n_e), F32)], axis=1)
    rb = jnp.concatenate([router_b[0], jnp.full((LANES - n_e,), -1e30, F32)])[None]
    x1, h2, meta, gates, cnt = _outproj(conv, attn, x[0], w_out[0].astype(BF16), mod2, norm_ffn, rw, rb,
                                        tm=256)

    out = _moe(x1, h2, meta, gates, cnt, g2, w_gate_up[0], b_gate_up[0], w_down[0], b_down[0])
    return out[None]


def _moe(x1, h2, meta, gates, cnt, g2, w_gate_up, b_gate_up, w_down, b_down):
    n_lat, d = h2.shape
    n_e = w_down.shape[0]
    counts = cnt[0, :n_e].astype(jnp.int32)
    top_idx = meta[:, :TOP_K]
    pos = meta[:, TOP_K:2 * TOP_K]
    nblk_e = (counts + ROW_BLK - 1) // ROW_BLK
    blk_end = jnp.cumsum(nblk_e)
    pad_start = (blk_end - nblk_e) * ROW_BLK
    dest = pad_start[top_idx] + pos
    n_blocks = (n_lat * TOP_K) // ROW_BLK + n_e
    n_rows = n_blocks * ROW_BLK
    tok = jnp.broadcast_to(jnp.arange(n_lat, dtype=jnp.int32)[:, None], (n_lat, TOP_K))
    row_tok = jnp.zeros((n_rows,), jnp.int32).at[dest.reshape(-1)].set(tok.reshape(-1))
    n_used = blk_end[-1:].astype(jnp.int32)
    nsb_e = (nblk_e + SB_BLOCKS - 1) // SB_BLOCKS
    sb_end = jnp.cumsum(nsb_e)
    n_sb = -(-n_blocks // SB_BLOCKS) + n_e
    j = jnp.arange(n_sb)
    sb_e = jnp.minimum(jnp.sum(sb_end[None, :] <= j[:, None], axis=1), n_e - 1).astype(jnp.int32)
    local = j - (sb_end - nsb_e)[sb_e]
    sb_b0 = ((blk_end - nblk_e)[sb_e] + local * SB_BLOCKS).astype(jnp.int32)
    sb_nb = jnp.clip(nblk_e[sb_e] - local * SB_BLOCKS, 0, SB_BLOCKS).astype(jnp.int32)
    n_sup = sb_end[-1:].astype(jnp.int32)
    sb_b0 = jnp.where(j < n_sup[0], sb_b0, 0)
    sb_nb = jnp.where(j < n_sup[0], sb_nb, 0)

    xb = _gather_rows(row_tok, h2, nb=256)
    yb = _experts(sb_e, sb_b0, sb_nb, n_sup, n_used, xb, w_gate_up, b_gate_up, w_down, b_down)

    tm_c = 128
    dest_km = dest.reshape(n_lat // tm_c, tm_c, TOP_K).transpose(0, 2, 1).reshape(n_lat // tm_c, 1,
                                                                                  TOP_K * tm_c)
    return _combine(dest_km, yb, x1, gates, g2, tm=tm_c)
```

```python
import functools
import math

import jax
import jax.numpy as jnp
from jax import lax
from jax.experimental import pallas as pl
from jax.experimental.pallas import tpu as pltpu

F32 = jnp.float32
BF16 = jnp.bfloat16

EPS = 1e-6
N_MOD = 6
GRID_W = 64
CONV_CH = 1024
CONV_GROUPS = 8
CONV_WIDTH = 31
N_HEADS = 16
Q_LORA = 512
KV_LORA = 256
QK_NOPE = 64
QK_ROPE = 32
V_DIM = 64
QK_DIM = QK_NOPE + QK_ROPE
SCALE = QK_DIM ** -0.5
ROPE_THETA = 10000.0
AXIS_ROPE = QK_ROPE // 2
AXIS_PAIRS = AXIS_ROPE // 2
Q0 = 2 * CONV_CH
KV0 = Q0 + Q_LORA
KR0 = KV0 + KV_LORA
N_EXPERTS = 32
TOP_K = 4
SWIGLU_LIMIT = 7.0
SWIGLU_ALPHA = 1.702

LANES = 128
HEAD_W = LANES
ROW_BLK = 128
VMEM_LIMIT = 56 * 1024 * 1024

_ROPE_PERM = tuple(ax * AXIS_ROPE + ab * AXIS_PAIRS + p
                   for ab in range(2) for ax in range(2) for p in range(AXIS_PAIRS))


def _cparams(sem, limit=VMEM_LIMIT):
    return pltpu.CompilerParams(dimension_semantics=sem, vmem_limit_bytes=limit)


def _const_spec(shape):
    nd = len(shape)
    return pl.BlockSpec(shape, lambda *_: (0,) * nd)


def _ada_kernel(c_ref, w_ref, b_ref, o_ref, *, rc):
    d, tn = w_ref.shape

    def body(k, carry):
        a0, a1 = carry
        r0 = pl.multiple_of(k * rc, rc)
        cc = c_ref[pl.ds(r0, rc), :]
        s = cc * jax.nn.sigmoid(cc)
        w = w_ref[pl.ds(r0, rc), :]
        p0 = (w * s[:, 0:1]).reshape(rc // 8, 8, tn).sum(axis=0)
        p1 = (w * s[:, 1:2]).reshape(rc // 8, 8, tn).sum(axis=0)
        return a0 + p0, a1 + p1

    z = jnp.zeros((8, tn), F32)
    a0, a1 = lax.fori_loop(0, d // rc, body, (z, z))
    b = b_ref[...]
    o_ref[0:1, :] = jnp.sum(a0, axis=0, keepdims=True) + b
    o_ref[1:2, :] = jnp.sum(a1, axis=0, keepdims=True) + b


def _ada(c2, w_ada, b_ada):
    d, n = w_ada.shape
    tn = 1024
    return pl.pallas_call(
        functools.partial(_ada_kernel, rc=128),
        out_shape=jax.ShapeDtypeStruct((2, n), F32),
        grid=(n // tn,),
        in_specs=[_const_spec((d, 2)),
                  pl.BlockSpec((d, tn), lambda j: (0, j)),
                  pl.BlockSpec((1, tn), lambda j: (0, j))],
        out_specs=pl.BlockSpec((2, tn), lambda j: (0, j)),
        compiler_params=_cparams(("arbitrary",)),
        name="ada",
    )(c2, w_ada, b_ada.reshape(1, n))


def _head_norm_rope(t, gain, cos, s_up, s_dn):
    r = lax.rsqrt(jnp.sum(t * t, axis=-1, keepdims=True) * (1.0 / QK_DIM) + EPS)
    t = t * r * gain
    return t * cos + pltpu.roll(t, AXIS_ROPE, 1) * s_up + pltpu.roll(t, HEAD_W - AXIS_ROPE, 1) * s_dn


def _inproj_kernel(x_ref, mod_ref, g_ref, win_ref, qa_ref, wq_ref, kva_ref, wkv_ref, gq_ref, gk_ref,
                   qadd_ref, kadd_ref, vone_ref, cos_ref, sup_ref, sdn_ref, glu_ref, q_ref, k_ref, vt_ref):
    x = x_ref[...]
    ms = jnp.mean(x * x, axis=-1, keepdims=True)
    y = x * lax.rsqrt(ms + EPS) * g_ref[...]
    h = y * (1.0 + mod_ref[0, 1:2, :]) + mod_ref[0, 0:1, :]
    proj = jnp.dot(h.astype(BF16), win_ref[...], preferred_element_type=F32)
    glu_ref[...] = proj[:, :CONV_CH] * jax.nn.sigmoid(proj[:, CONV_CH:Q0])

    pq = proj[:, Q0:KV0]
    pq = pq * lax.rsqrt(jnp.mean(pq * pq, axis=-1, keepdims=True) + EPS) * qa_ref[...]
    q = jnp.dot(pq.astype(BF16), wq_ref[...], preferred_element_type=F32)
    pkv = proj[:, KV0:KR0]
    pkv = pkv * lax.rsqrt(jnp.mean(pkv * pkv, axis=-1, keepdims=True) + EPS) * kva_ref[...]
    kv = jnp.dot(pkv.astype(BF16), wkv_ref[...], preferred_element_type=F32)
    kr = proj[:, KR0:KR0 + HEAD_W]

    cos, s_up, s_dn = cos_ref[...], sup_ref[...], sdn_ref[...]
    gq, gk = gq_ref[...], gk_ref[...]
    qadd, kadd = qadd_ref[...], kadd_ref[...]
    for hd in range(N_HEADS):
        sl = slice(hd * HEAD_W, (hd + 1) * HEAD_W)
        q_ref[:, sl] = (_head_norm_rope(q[:, sl], gq, cos, s_up, s_dn) + qadd).astype(BF16)
        k_ref[:, sl] = (_head_norm_rope(kv[:, sl] + kr, gk, cos, s_up, s_dn) + kadd).astype(BF16)
    vt_ref[...] = (kv[:, N_HEADS * HEAD_W:] + vone_ref[...]).T.astype(BF16)


def _inproj(xin, mod1, g_mix, win, qa, wq, kva, wkv, gq, gk, qadd, kadd, vone, cos, s_up, s_dn, n_lat, tm):
    t_all, d = xin.shape
    n_lat_tiles = n_lat // tm
    hw = N_HEADS * HEAD_W
    row = lambda w: pl.BlockSpec((tm, w), lambda i: (i, 0))
    return pl.pallas_call(
        _inproj_kernel,
        out_shape=(jax.ShapeDtypeStruct((t_all, CONV_CH), F32),
                   jax.ShapeDtypeStruct((t_all, hw), BF16),
                   jax.ShapeDtypeStruct((t_all, hw), BF16),
                   jax.ShapeDtypeStruct((hw, t_all), BF16)),
        grid=(t_all // tm,),
        in_specs=[row(d),
                  pl.BlockSpec((1, 2, d), lambda i: (jnp.where(i < n_lat_tiles, 0, 1), 0, 0)),
                  _const_spec(g_mix.shape), _const_spec(win.shape), _const_spec(qa.shape),
                  _const_spec(wq.shape), _const_spec(kva.shape), _const_spec(wkv.shape),
                  _const_spec(gq.shape), _const_spec(gk.shape), _const_spec(qadd.shape),
                  _const_spec(kadd.shape), _const_spec(vone.shape),
                  row(HEAD_W), row(HEAD_W), row(HEAD_W)],
        out_specs=(row(CONV_CH), row(hw), row(hw), pl.BlockSpec((hw, tm), lambda i: (0, i))),
        compiler_params=_cparams(("arbitrary",)),
        name="inproj",
    )(xin, mod1, g_mix, win, qa, wq, kva, wkv, gq, gk, qadd, kadd, vone, cos, s_up, s_dn)


CONV_HALO = 16
CONV_RC = 64


def _conv_kernel(prev_ref, main_ref, next_ref, w_ref, b_ref, g_ref, beta_ref, o_ref, buf_ref):
    i = pl.program_id(0)
    tc = main_ref.shape[0]
    gw = CONV_CH // CONV_GROUPS
    buf_ref[0:CONV_HALO, :] = jnp.where(i > 0, prev_ref[...], 0.0)
    buf_ref[CONV_HALO:CONV_HALO + tc, :] = main_ref[...]
    buf_ref[CONV_HALO + tc:, :] = jnp.where(i < pl.num_programs(0) - 1, next_ref[...], 0.0)
    win = CONV_RC + 2 * CONV_HALO

    def chunk(c, carry):
        r0 = pl.multiple_of(c * CONV_RC, CONV_RC)
        for g in range(CONV_GROUPS):
            ls = slice(g * gw, (g + 1) * gw)
            w_all = buf_ref[pl.ds(r0, win), ls]
            acc = jnp.zeros((CONV_RC, gw), F32)
            for b in range(8):
                wb = w_all[b:b + CONV_RC + 24, :]
                for a in range(4):
                    k = 8 * a + b - 1
                    if 0 <= k < CONV_WIDTH:
                        acc = acc + wb[8 * a:8 * a + CONV_RC, :] * w_ref[k:k + 1, ls]
            acc = acc + b_ref[:, ls]
            mu = jnp.mean(acc, axis=-1, keepdims=True)
            cen = acc - mu
            var = jnp.mean(cen * cen, axis=-1, keepdims=True)
            yv = cen * lax.rsqrt(var + EPS) * g_ref[:, ls] + beta_ref[:, ls]
            o_ref[pl.ds(r0, CONV_RC), ls] = (yv * jax.nn.sigmoid(yv)).astype(o_ref.dtype)
        return carry

    lax.fori_loop(0, tc // CONV_RC, chunk, 0)


def _conv(glu, w, b, g, beta, n_lat, tc):
    hb = tc // CONV_HALO
    return pl.pallas_call(
        _conv_kernel,
        out_shape=jax.ShapeDtypeStruct((n_lat, CONV_CH), BF16),
        grid=(n_lat // tc,),
        in_specs=[pl.BlockSpec((CONV_HALO, CONV_CH), lambda i: (jnp.maximum(i * hb - 1, 0), 0)),
                  pl.BlockSpec((tc, CONV_CH), lambda i: (i, 0)),
                  pl.BlockSpec((CONV_HALO, CONV_CH), lambda i: ((i + 1) * hb, 0)),
                  _const_spec(w.shape), _const_spec(b.shape), _const_spec(g.shape),
                  _const_spec(beta.shape)],
        out_specs=pl.BlockSpec((tc, CONV_CH), lambda i: (i, 0)),
        scratch_shapes=[pltpu.VMEM((tc + 2 * CONV_HALO, CONV_CH), F32)],
        compiler_params=_cparams(("arbitrary",)),
        name="conv",
    )(glu, glu, glu, w, b, g, beta)


ATTN_HEADS_PER_STEP = 2
SHIFT_LIMIT = 40.0


def _attn_fixed_kernel(q_ref, k_ref, vt_ref, o_ref, *, tk):
    tq = q_ref.shape[0]
    n_kv = k_ref.shape[0] // tk
    qs = [q_ref[:, hh * HEAD_W:(hh + 1) * HEAD_W] for hh in range(ATTN_HEADS_PER_STEP)]

    def step(j, carry):
        r0 = pl.multiple_of(j * tk, tk)
        out = []
        for hh in range(ATTN_HEADS_PER_STEP):
            kc = k_ref[pl.ds(r0, tk), hh * HEAD_W:(hh + 1) * HEAD_W]
            vtc = vt_ref[hh * HEAD_W:(hh + 1) * HEAD_W, pl.ds(r0, tk)]
            st = lax.dot_general(kc, qs[hh], (((1,), (1,)), ((), ())), preferred_element_type=F32)
            pt = jnp.exp2(st).astype(BF16)
            out.append(carry[hh] + jnp.dot(vtc, pt, preferred_element_type=F32))
        return tuple(out)

    init = tuple(jnp.zeros((HEAD_W, tq), F32) for _ in range(ATTN_HEADS_PER_STEP))
    res = lax.fori_loop(0, n_kv, step, init)
    outs = [(acc[:V_DIM, :] / acc[V_DIM:V_DIM + 1, :]).T for acc in res]
    o_ref[...] = jnp.concatenate(outs, axis=-1).astype(o_ref.dtype)


def _attn_online_kernel(q_ref, k_ref, vt_ref, o_ref, *, tk):
    tq = q_ref.shape[0]
    n_kv = k_ref.shape[0] // tk
    qs = [q_ref[:, hh * HEAD_W:(hh + 1) * HEAD_W] for hh in range(ATTN_HEADS_PER_STEP)]

    def step(j, carry):
        r0 = pl.multiple_of(j * tk, tk)
        out = []
        for hh in range(ATTN_HEADS_PER_STEP):
            m, acc = carry[hh]
            kc = k_ref[pl.ds(r0, tk), hh * HEAD_W:(hh + 1) * HEAD_W]
            vtc = vt_ref[hh * HEAD_W:(hh + 1) * HEAD_W, pl.ds(r0, tk)]
            st = lax.dot_general(kc, qs[hh], (((1,), (1,)), ((), ())), preferred_element_type=F32)
            m_new = jnp.maximum(m, jnp.max(st, axis=0, keepdims=True))
            pt = jnp.exp2(st - m_new).astype(BF16)
            acc = acc * jnp.exp2(m - m_new) + jnp.dot(vtc, pt, preferred_element_type=F32)
            out.append((m_new, acc))
        return tuple(out)

    init = tuple((jnp.full((1, tq), -jnp.inf, F32), jnp.zeros((HEAD_W, tq), F32))
                 for _ in range(ATTN_HEADS_PER_STEP))
    res = lax.fori_loop(0, n_kv, step, init)
    outs = [(acc[:V_DIM, :] / acc[V_DIM:V_DIM + 1, :]).T for _, acc in res]
    o_ref[...] = jnp.concatenate(outs, axis=-1).astype(o_ref.dtype)


def _attn(body, q, k, vt, n_lat, tq, tk):
    t_all = k.shape[0]
    pw = ATTN_HEADS_PER_STEP * HEAD_W
    return pl.pallas_call(
        functools.partial(body, tk=tk),
        out_shape=jax.ShapeDtypeStruct((n_lat, N_HEADS * V_DIM), BF16),
        grid=(N_HEADS // ATTN_HEADS_PER_STEP, n_lat // tq),
        in_specs=[pl.BlockSpec((tq, pw), lambda hp, i: (i, hp)),
                  pl.BlockSpec((t_all, pw), lambda hp, i: (0, hp)),
                  pl.BlockSpec((pw, t_all), lambda hp, i: (hp, 0))],
        out_specs=pl.BlockSpec((tq, ATTN_HEADS_PER_STEP * V_DIM), lambda hp, i: (i, hp)),
        compiler_params=_cparams(("arbitrary", "arbitrary")),
        name="attn",
    )(q, k, vt)


def _outproj_kernel(conv_ref, attn_ref, x_ref, wo_ref, mod_ref, g_ref, rw_ref, rb_ref,
                    x1_ref, h2_ref, meta_ref, gate_ref, cnt_ref, cnt_acc):
    i = pl.program_id(0)
    tm = x_ref.shape[0]

    @pl.when(i == 0)
    def _():
        cnt_acc[...] = jnp.zeros_like(cnt_acc)

    y = jnp.dot(conv_ref[...], wo_ref[:CONV_CH, :], preferred_element_type=F32)
    y = y + jnp.dot(attn_ref[...], wo_ref[CONV_CH:, :], preferred_element_type=F32)
    x1 = x_ref[...] + mod_ref[0:1, :] * y
    x1_ref[...] = x1
    ms = jnp.mean(x1 * x1, axis=-1, keepdims=True)
    h2 = x1 * lax.rsqrt(ms + EPS) * g_ref[...]
    h2 = h2 * (1.0 + mod_ref[2:3, :]) + mod_ref[1:2, :]
    h2_ref[...] = h2

    h_hi = h2.astype(BF16)
    h_lo = (h2 - h_hi.astype(F32)).astype(BF16)
    rw = rw_ref[...]
    r_hi = rw.astype(BF16)
    r_lo = (rw - r_hi.astype(F32)).astype(BF16)
    logits = (jnp.dot(h_hi, r_hi, preferred_element_type=F32)
              + jnp.dot(h_lo, r_hi, preferred_element_type=F32)
              + jnp.dot(h_hi, r_lo, preferred_element_type=F32)) + rb_ref[...]

    lane = lax.broadcasted_iota(jnp.int32, (tm, LANES), 1)
    work = logits
    vals, idxs, sels = [], [], []
    for _ in range(TOP_K):
        mx = jnp.max(work, axis=-1, keepdims=True)
        idx = jnp.min(jnp.where(work == mx, lane, LANES), axis=-1, keepdims=True)
        sel = lane == idx
        vals.append(mx)
        idxs.append(idx)
        sels.append(sel)
        work = jnp.where(sel, -jnp.inf, work)
    es = [jnp.exp(v - vals[0]) for v in vals]
    denom = es[0] + es[1] + es[2] + es[3]

    chosen = jnp.zeros((tm, LANES), F32)
    for sel in sels:
        chosen = chosen + jnp.where(sel, 1.0, 0.0)
    rr = lax.broadcasted_iota(jnp.int32, (tm, tm), 0)
    cc = lax.broadcasted_iota(jnp.int32, (tm, tm), 1)
    tri = jnp.where(cc < rr, 1.0, 0.0).astype(BF16)
    before = jnp.dot(tri, chosen.astype(BF16), preferred_element_type=F32) + cnt_acc[...]

    meta = jnp.zeros((tm, LANES), jnp.int32)
    gates = jnp.zeros((tm, LANES), F32)
    for kk in range(TOP_K):
        pos = jnp.sum(jnp.where(sels[kk], before, 0.0), axis=-1, keepdims=True).astype(jnp.int32)
        meta = jnp.where(lane == kk, idxs[kk], meta)
        meta = jnp.where(lane == TOP_K + kk, pos, meta)
        gates = jnp.where(lane == kk, es[kk] / denom, gates)
    meta_ref[...] = meta
    gate_ref[...] = gates
    cnt_acc[...] = cnt_acc[...] + jnp.sum(chosen, axis=0, keepdims=True)
    cnt_ref[...] = cnt_acc[...]


def _outproj(conv, attn, x, wo, mod2, g_ffn, rw, rb, tm):
    n_lat, d = x.shape
    row = lambda w: pl.BlockSpec((tm, w), lambda i: (i, 0))
    return pl.pallas_call(
        _outproj_kernel,
        out_shape=(jax.ShapeDtypeStruct((n_lat, d), F32),
                   jax.ShapeDtypeStruct((n_lat, d), F32),
                   jax.ShapeDtypeStruct((n_lat, LANES), jnp.int32),
                   jax.ShapeDtypeStruct((n_lat, LANES), F32),
                   jax.ShapeDtypeStruct((1, LANES), F32)),
        grid=(n_lat // tm,),
        in_specs=[row(CONV_CH), row(N_HEADS * V_DIM), row(d), _const_spec(wo.shape),
                  _const_spec(mod2.shape), _const_spec(g_ffn.shape), _const_spec(rw.shape),
                  _const_spec(rb.shape)],
        out_specs=(row(d), row(d), row(LANES), row(LANES), _const_spec((1, LANES))),
        scratch_shapes=[pltpu.VMEM((1, LANES), F32)],
        compiler_params=_cparams(("arbitrary",)),
        name="outproj",
    )(conv, attn, x, wo, mod2, g_ffn, rw, rb)


ROW_DMA_UNROLL = 8


def _fetch_rows(idx_first_ref, idx_next_ref, src_hbm, buf, sem):
    i = pl.program_id(0)
    n = pl.num_programs(0)
    n_rows = buf.shape[1]
    slot = lax.rem(i, 2)

    def row_copy(idx_ref, sl, r):
        return pltpu.make_async_copy(src_hbm.at[pl.ds(idx_ref[0, 0, r], 1)], buf.at[sl, pl.ds(r, 1)],
                                     sem.at[sl])

    def request(idx_ref, sl):
        def body(r, c):
            row_copy(idx_ref, sl, r).start()
            return c
        lax.fori_loop(0, n_rows, body, 0, unroll=ROW_DMA_UNROLL)

    @pl.when(i == 0)
    def _():
        request(idx_first_ref, 0)

    @pl.when(i + 1 < n)
    def _():
        request(idx_next_ref, 1 - slot)

    def arrive(r, c):
        row_copy(idx_first_ref, slot, r).wait()
        return c
    lax.fori_loop(0, n_rows, arrive, 0, unroll=ROW_DMA_UNROLL)
    return slot


def _idx_specs(n_per_step, n_steps):
    first = pl.BlockSpec((1, 1, n_per_step), lambda i: (0, 0, 0), memory_space=pltpu.SMEM)
    nxt = pl.BlockSpec((1, 1, n_per_step), lambda i: (jnp.minimum(i + 1, n_steps - 1), 0, 0),
                       memory_space=pltpu.SMEM)
    return first, nxt


def _gather_kernel(tok_first_ref, tok_next_ref, h_hbm, o_ref, buf, sem):
    slot = _fetch_rows(tok_first_ref, tok_next_ref, h_hbm, buf, sem)
    o_ref[...] = buf[slot].astype(o_ref.dtype)


def _gather_rows(row_tok, h2, nb):
    n_rows = row_tok.shape[0]
    d = h2.shape[1]
    n_steps = n_rows // nb
    toks = row_tok.reshape(n_steps, 1, nb)
    first, nxt = _idx_specs(nb, n_steps)
    return pl.pallas_call(
        _gather_kernel,
        out_shape=jax.ShapeDtypeStruct((n_rows, d), BF16),
        grid=(n_steps,),
        in_specs=[first, nxt, pl.BlockSpec(memory_space=pl.ANY)],
        out_specs=pl.BlockSpec((nb, d), lambda i: (i, 0)),
        scratch_shapes=[pltpu.VMEM((2, nb, d), F32), pltpu.SemaphoreType.DMA((2,))],
        compiler_params=_cparams(("arbitrary",)),
        name="gather",
    )(toks, toks, h2)


SB_BLOCKS = 12
EXPERT_TF = 256
EXPERT_GROUP = 4


def _expert_kernel(sbe_ref, sbb0_ref, sbnb_ref, nsup_ref, nused_ref,
                   xs_hbm, wg_ref, wl_ref, wd_ref, bg_ref, bl_ref, bd_ref, ys_hbm,
                   xbuf, acc, wgl_s, wd_s, zbuf, xsem, ysem):
    s = pl.program_id(0)
    f = pl.program_id(1)
    nf = pl.num_programs(1)
    tf = wd_s.shape[0]
    n_sup = nsup_ref[0]
    n_blocks = ys_hbm.shape[0] // ROW_BLK

    def x_copy(sb, slot, b):
        row0 = pl.multiple_of((sbb0_ref[sb] + b) * ROW_BLK, ROW_BLK)
        return pltpu.make_async_copy(xs_hbm.at[pl.ds(row0, ROW_BLK)],
                                     xbuf.at[slot, pl.ds(pl.multiple_of(b * ROW_BLK, ROW_BLK), ROW_BLK)],
                                     xsem.at[slot])

    def y_copy(sb, b):
        row0 = pl.multiple_of((sbb0_ref[sb] + b) * ROW_BLK, ROW_BLK)
        return pltpu.make_async_copy(acc.at[pl.ds(pl.multiple_of(b * ROW_BLK, ROW_BLK), ROW_BLK)],
                                     ys_hbm.at[pl.ds(row0, ROW_BLK)], ysem)

    def z_copy(b):
        return pltpu.make_async_copy(zbuf, ys_hbm.at[pl.ds(pl.multiple_of(b * ROW_BLK, ROW_BLK), ROW_BLK)],
                                     ysem)

    def for_blocks(n, fn):
        def body(b, c):
            fn(b)
            return c
        lax.fori_loop(0, n, body, 0)

    @pl.when(s < n_sup)
    def _():
        slot = lax.rem(s, 2)
        nb = sbnb_ref[s]

        @pl.when(f == 0)
        def _():
            @pl.when(s == 0)
            def _():
                for_blocks(nb, lambda b: x_copy(0, 0, b).start())

            for_blocks(nb, lambda b: x_copy(s, slot, b).wait())

            @pl.when(s + 1 < n_sup)
            def _():
                nxt = jnp.minimum(s + 1, n_sup - 1)
                for_blocks(sbnb_ref[nxt], lambda b: x_copy(nxt, 1 - slot, b).start())

        wgl_s[:, :tf] = wg_ref[0].astype(BF16)
        wgl_s[:, tf:] = wl_ref[0].astype(BF16)
        wd_s[...] = wd_ref[0].astype(BF16)
        bg, bl = bg_ref[0], bl_ref[0]

        @pl.when(f == 0)
        def _():
            bias = jnp.broadcast_to(bd_ref[0], (ROW_BLK, acc.shape[1]))

            def init(b):
                acc[pl.ds(pl.multiple_of(b * ROW_BLK, ROW_BLK), ROW_BLK), :] = bias
            for_blocks(nb, init)

        def add_tile(b, m):
            rows = m * ROW_BLK
            r0 = pl.multiple_of(b * ROW_BLK, ROW_BLK)
            gu = jnp.dot(xbuf[slot, pl.ds(r0, rows), :], wgl_s[...], preferred_element_type=F32)
            glu = jnp.minimum(gu[:, :tf] + bg, SWIGLU_LIMIT)
            lin = jnp.clip(gu[:, tf:] + bl, -SWIGLU_LIMIT, SWIGLU_LIMIT)
            hmid = glu * jax.nn.sigmoid(SWIGLU_ALPHA * glu) * (lin + 1.0)
            acc[pl.ds(r0, rows), :] += jnp.dot(hmid.astype(BF16), wd_s[...], preferred_element_type=F32)

            @pl.when(f == nf - 1)
            def _():
                for u in range(m):
                    y_copy(s, b + u).start()

        n_grp = nb // EXPERT_GROUP
        for_blocks(n_grp, lambda g: add_tile(g * EXPERT_GROUP, EXPERT_GROUP))
        done = n_grp * EXPERT_GROUP
        m = EXPERT_GROUP // 2
        while m >= 1:
            take = ((nb - done) // m) > 0

            @pl.when(take)
            def _(m=m, done=done):
                add_tile(done, m)
            done = done + jnp.where(take, m, 0)
            m //= 2

        @pl.when(f == nf - 1)
        def _():
            for_blocks(nb, lambda b: y_copy(s, b).wait())

    @pl.when((s == pl.num_programs(0) - 1) & (f == nf - 1))
    def _():
        zbuf[...] = jnp.zeros_like(zbuf)
        n_used = nused_ref[0]

        def tail(fn):
            def body(b, c):
                fn(b)
                return c
            lax.fori_loop(n_used, n_blocks, body, 0)

        tail(lambda b: z_copy(b).start())
        tail(lambda b: z_copy(b).wait())


def _experts(sb_e, sb_b0, sb_nb, n_sup, n_used, xs, w_gate_up, b_gate_up, w_down, b_down):
    n_rows, d = xs.shape
    tf = EXPERT_TF
    de = w_down.shape[1]
    nf = de // tf
    n_e = w_down.shape[0]
    n_sb = sb_e.shape[0]
    rs = SB_BLOCKS * ROW_BLK
    bgu = b_gate_up.reshape(n_e, 1, 2 * de)
    bdn = b_down.reshape(n_e, 1, d)

    def e_of(s, nsup, sbe):
        return sbe[jnp.minimum(s, nsup[0] - 1)]

    def f_of(s, f, nsup):
        return jnp.where(s < nsup[0], f, nf - 1)

    grid_spec = pltpu.PrefetchScalarGridSpec(
        num_scalar_prefetch=5,
        grid=(n_sb, nf),
        in_specs=[
            pl.BlockSpec(memory_space=pl.ANY),
            pl.BlockSpec((1, d, tf), lambda s, f, sbe, b0, nb, ns, nu: (e_of(s, ns, sbe), 0, f_of(s, f, ns))),
            pl.BlockSpec((1, d, tf), lambda s, f, sbe, b0, nb, ns, nu: (e_of(s, ns, sbe), 0, nf + f_of(s, f, ns))),
            pl.BlockSpec((1, tf, d), lambda s, f, sbe, b0, nb, ns, nu: (e_of(s, ns, sbe), f_of(s, f, ns), 0)),
            pl.BlockSpec((1, 1, tf), lambda s, f, sbe, b0, nb, ns, nu: (e_of(s, ns, sbe), 0, f_of(s, f, ns))),
            pl.BlockSpec((1, 1, tf), lambda s, f, sbe, b0, nb, ns, nu: (e_of(s, ns, sbe), 0, nf + f_of(s, f, ns))),
            pl.BlockSpec((1, 1, d), lambda s, f, sbe, b0, nb, ns, nu: (e_of(s, ns, sbe), 0, 0)),
        ],
        out_specs=pl.BlockSpec(memory_space=pl.ANY),
        scratch_shapes=[pltpu.VMEM((2, rs, d), BF16), pltpu.VMEM((rs, d), F32),
                        pltpu.VMEM((d, 2 * tf), BF16), pltpu.VMEM((tf, d), BF16),
                        pltpu.VMEM((ROW_BLK, d), F32),
                        pltpu.SemaphoreType.DMA((2,)), pltpu.SemaphoreType.DMA],
    )
    return pl.pallas_call(
        _expert_kernel,
        out_shape=jax.ShapeDtypeStruct((n_rows, d), F32),
        grid_spec=grid_spec,
        compiler_params=_cparams(("arbitrary", "arbitrary")),
        name="experts",
    )(sb_e, sb_b0, sb_nb, n_sup, n_used, xs, w_gate_up, w_gate_up, w_down, bgu, bgu, bdn)


def _combine_kernel(dest_first_ref, dest_next_ref, y_hbm, x1_ref, gate_ref, g2_ref, o_ref, buf, sem):
    tm = x1_ref.shape[0]
    slot = _fetch_rows(dest_first_ref, dest_next_ref, y_hbm, buf, sem)
    gates = gate_ref[...]
    acc = buf[slot, 0:tm, :] * gates[:, 0:1]
    for kk in range(1, TOP_K):
        acc = acc + buf[slot, kk * tm:(kk + 1) * tm, :] * gates[:, kk:kk + 1]
    o_ref[...] = x1_ref[...] + g2_ref[...] * acc


def _combine(dest_km, yb, x1, gates, g2, tm):
    n_lat, d = x1.shape
    n_steps = n_lat // tm
    first, nxt = _idx_specs(TOP_K * tm, n_steps)
    return pl.pallas_call(
        _combine_kernel,
        out_shape=jax.ShapeDtypeStruct((n_lat, d), F32),
        grid=(n_steps,),
        in_specs=[first, nxt,
                  pl.BlockSpec(memory_space=pl.ANY),
                  pl.BlockSpec((tm, d), lambda i: (i, 0)),
                  pl.BlockSpec((tm, LANES), lambda i: (i, 0)),
                  _const_spec((1, d))],
        out_specs=pl.BlockSpec((tm, d), lambda i: (i, 0)),
        scratch_shapes=[pltpu.VMEM((2, TOP_K * tm, d), F32), pltpu.SemaphoreType.DMA((2,))],
        compiler_params=_cparams(("arbitrary",)),
        name="combine",
    )(dest_km, dest_km, yb, x1, gates, g2)


def _head_cols(w, per_head, start, width, perm=None):
    k = w.shape[0]
    t = w.reshape(k, N_HEADS, per_head)[:, :, start:start + width]
    return t if perm is None else t[:, :, jnp.asarray(perm)]


def _head_layout(nope, rope):
    k = nope.shape[0]
    parts = [nope]
    if rope is not None:
        parts.append(rope)
    used = sum(p.shape[-1] for p in parts)
    parts.append(jnp.zeros((k, N_HEADS, HEAD_W - used), nope.dtype))
    return jnp.concatenate(parts, axis=-1).reshape(k, N_HEADS * HEAD_W)


def _gain_row(g, mult=1.0):
    perm = jnp.asarray(_ROPE_PERM)
    return jnp.concatenate([g[:QK_NOPE], g[QK_NOPE:][perm], jnp.zeros((HEAD_W - QK_DIM,), F32)])[None] * mult


def _rope_tables(n_lat, n_ctx):
    rows = n_lat // GRID_W
    row = jnp.broadcast_to(jnp.arange(rows, dtype=F32)[:, None], (rows, GRID_W)).reshape(-1)
    col = jnp.broadcast_to(jnp.arange(GRID_W, dtype=F32)[None, :], (rows, GRID_W)).reshape(-1)
    inv_freq = ROPE_THETA ** (-2.0 * jnp.arange(AXIS_PAIRS, dtype=F32) / AXIS_ROPE)
    ang = jnp.concatenate([row[:, None] * inv_freq, col[:, None] * inv_freq], axis=1)
    cs, sn = jnp.cos(ang), jnp.sin(ang)
    one = jnp.ones((n_lat, QK_NOPE), F32)
    zn = jnp.zeros((n_lat, QK_NOPE), F32)
    zp = jnp.zeros((n_lat, HEAD_W - QK_DIM), F32)
    za = jnp.zeros((n_lat, AXIS_ROPE), F32)
    cos = jnp.concatenate([one, cs, cs, zp], axis=1)
    s_up = jnp.concatenate([zn, za, sn, zp], axis=1)
    s_dn = jnp.concatenate([zn, -sn, za, zp], axis=1)
    pad = lambda t, fill: jnp.concatenate([t, jnp.full((n_ctx, HEAD_W), fill, F32)], axis=0)
    return pad(cos, 1.0), pad(s_up, 0.0), pad(s_dn, 0.0)


def kernel(x, c, ctx, c_ctx, w_ada, b_ada, norm_mix, norm_ffn, w_in, conv_w, conv_b, conv_norm_g, conv_norm_b, q_a_norm, w_q_up, kv_a_norm, w_kv_up, q_head_norm, k_head_norm, w_out, router_w, router_b, w_gate_up, b_gate_up, w_down, b_down):
    bsz, n_lat, d = x.shape
    n_ctx = ctx.shape[1]
    assert bsz == 1 and w_ada.shape[0] == 1
    n_e = router_w.shape[-1]
    perm = _ROPE_PERM

    mod = _ada(jnp.stack([c[0], c_ctx], axis=1), w_ada[0], b_ada[0])
    mod = mod.reshape(2, N_MOD, d)
    mod1 = mod[:, 0:2, :]
    mod2 = jnp.stack([mod[0, 2], mod[0, 3], mod[0, 4]], axis=0)
    g2 = mod[0, 5][None]

    wi = w_in[0]
    kr_cols = jnp.concatenate([jnp.zeros((d, QK_NOPE), F32), wi[:, KR0:][:, jnp.asarray(perm)],
                               jnp.zeros((d, HEAD_W - QK_DIM), F32)], axis=1)
    win = jnp.concatenate([wi[:, :KR0], kr_cols], axis=1).astype(BF16)
    wq = _head_layout(_head_cols(w_q_up[0], QK_DIM, 0, QK_NOPE),
                      _head_cols(w_q_up[0], QK_DIM, QK_NOPE, QK_ROPE, perm)).astype(BF16)
    wk = _head_layout(_head_cols(w_kv_up[0], QK_NOPE + V_DIM, 0, QK_NOPE), None)
    wv = _head_layout(_head_cols(w_kv_up[0], QK_NOPE + V_DIM, QK_NOPE, V_DIM), None)
    wkv = jnp.concatenate([wk, wv], axis=1).astype(BF16)
    gq = _gain_row(q_head_norm[0], SCALE * math.log2(math.e))
    gk = _gain_row(k_head_norm[0])
    vone = jnp.zeros((N_HEADS, HEAD_W), F32).at[:, V_DIM].set(1.0).reshape(1, N_HEADS * HEAD_W)
    cos, s_up, s_dn = _rope_tables(n_lat, n_ctx)

    bound = (QK_DIM * SCALE * math.log2(math.e) * 1.01) * (jnp.max(jnp.abs(q_head_norm[0]))
                                                          * jnp.max(jnp.abs(k_head_norm[0])))
    shift = jnp.ceil(bound * 4.0) * 0.25
    fixed_ok = shift <= SHIFT_LIMIT
    lane = jnp.arange(HEAD_W)[None]
    qadd = jnp.where(lane == QK_DIM, jnp.where(fixed_ok, -shift, 0.0), 0.0).astype(F32)
    kadd = jnp.where(lane == QK_DIM, 1.0, 0.0).astype(F32)

    xin = jnp.concatenate([x[0], ctx[0]], axis=0)
    glu, q, k, vt = _inproj(xin, mod1, norm_mix, win, q_a_norm, wq, kv_a_norm, wkv, gq, gk, qadd, kadd, vone,
                            cos, s_up, s_dn, n_lat, tm=256)
    conv = _conv(glu, conv_w[0], conv_b, conv_norm_g, conv_norm_b, n_lat, tc=512)
    t_all = n_lat + n_ctx
    tk = max(t for t in range(256, 1025, 256) if t_all % t == 0)
    tq = min(1024, n_lat)
    attn = lax.cond(fixed_ok,
                    lambda: _attn(_attn_fixed_kernel, q, k, vt, n_lat, tq, tk),
                    lambda: _attn(_attn_online_kernel, q, k, vt, n_lat, tq, tk))

    rw = jnp.concatenate([router_w[0], jnp.zeros((d, LANES - n_e), F32)], axis=1)
    rb = jnp.concatenate([router_b[0], jnp.full((LANES - n_e,), -1e30, F32)])[None]
    x1, h2, meta, gates, cnt = _outproj(conv, attn, x[0], w_out[0].astype(BF16), mod2, norm_ffn, rw, rb,
                                        tm=256)

    out = _moe(x1, h2, meta, gates, cnt, g2, w_gate_up[0], b_gate_up[0], w_down[0], b_down[0])
    return out[None]


def _moe(x1, h2, meta, gates, cnt, g2, w_gate_up, b_gate_up, w_down, b_down):
    n_lat, d = h2.shape
    n_e = w_down.shape[0]
    counts = cnt[0, :n_e].astype(jnp.int32)
    top_idx = meta[:, :TOP_K]
    pos = meta[:, TOP_K:2 * TOP_K]
    nblk_e = (counts + ROW_BLK - 1) // ROW_BLK
    blk_end = jnp.cumsum(nblk_e)
    pad_start = (blk_end - nblk_e) * ROW_BLK
    dest = pad_start[top_idx] + pos
    n_blocks = (n_lat * TOP_K) // ROW_BLK + n_e
    n_rows = n_blocks * ROW_BLK
    tok = jnp.broadcast_to(jnp.arange(n_lat, dtype=jnp.int32)[:, None], (n_lat, TOP_K))
    row_tok = jnp.zeros((n_rows,), jnp.int32).at[dest.reshape(-1)].set(tok.reshape(-1))
    n_used = blk_end[-1:].astype(jnp.int32)
    nsb_e = (nblk_e + SB_BLOCKS - 1) // SB_BLOCKS
    sb_end = jnp.cumsum(nsb_e)
    n_sb = -(-n_blocks // SB_BLOCKS) + n_e
    j = jnp.arange(n_sb)
    sb_e = jnp.minimum(jnp.sum(sb_end[None, :] <= j[:, None], axis=1), n_e - 1).astype(jnp.int32)
    local = j - (sb_end - nsb_e)[sb_e]
    sb_b0 = ((blk_end - nblk_e)[sb_e] + local * SB_BLOCKS).astype(jnp.int32)
    sb_nb = jnp.clip(nblk_e[sb_e] - local * SB_BLOCKS, 0, SB_BLOCKS).astype(jnp.int32)
    n_sup = sb_end[-1:].astype(jnp.int32)
    sb_b0 = jnp.where(j < n_sup[0], sb_b0, 0)
    sb_nb = jnp.where(j < n_sup[0], sb_nb, 0)

    xb = _gather_rows(row_tok, h2, nb=256)
    yb = _experts(sb_e, sb_b0, sb_nb, n_sup, n_used, xb, w_gate_up, b_gate_up, w_down, b_down)

    tm_c = 128
    dest_km = dest.reshape(n_lat // tm_c, tm_c, TOP_K).transpose(0, 2, 1).reshape(n_lat // tm_c, 1,
                                                                                  TOP_K * tm_c)
    return _combine(dest_km, yb, x1, gates, g2, tm=tm_c)
```

```python
import functools
import math

import jax
import jax.numpy as jnp
from jax import lax
from jax.experimental import pallas as pl
from jax.experimental.pallas import tpu as pltpu

F32 = jnp.float32
BF16 = jnp.bfloat16

EPS = 1e-6
N_MOD = 6
GRID_W = 64
CONV_CH = 1024
CONV_GROUPS = 8
CONV_WIDTH = 31
N_HEADS = 16
Q_LORA = 512
KV_LORA = 256
QK_NOPE = 64
QK_ROPE = 32
V_DIM = 64
QK_DIM = QK_NOPE + QK_ROPE
SCALE = QK_DIM ** -0.5
ROPE_THETA = 10000.0
AXIS_ROPE = QK_ROPE // 2
AXIS_PAIRS = AXIS_ROPE // 2
Q0 = 2 * CONV_CH
KV0 = Q0 + Q_LORA
KR0 = KV0 + KV_LORA
N_EXPERTS = 32
TOP_K = 4
SWIGLU_LIMIT = 7.0
SWIGLU_ALPHA = 1.702

LANES = 128
HEAD_W = LANES
ROW_BLK = 128
VMEM_LIMIT = 56 * 1024 * 1024

_ROPE_PERM = tuple(ax * AXIS_ROPE + ab * AXIS_PAIRS + p
                   for ab in range(2) for ax in range(2) for p in range(AXIS_PAIRS))


def _cparams(sem, limit=VMEM_LIMIT):
    return pltpu.CompilerParams(dimension_semantics=sem, vmem_limit_bytes=limit)


def _const_spec(shape):
    nd = len(shape)
    return pl.BlockSpec(shape, lambda *_: (0,) * nd)


def _ada_kernel(c_ref, w_ref, b_ref, o_ref, *, rc):
    d, tn = w_ref.shape

    def body(k, carry):
        a0, a1 = carry
        r0 = pl.multiple_of(k * rc, rc)
        cc = c_ref[pl.ds(r0, rc), :]
        s = cc * jax.nn.sigmoid(cc)
        w = w_ref[pl.ds(r0, rc), :]
        p0 = (w * s[:, 0:1]).reshape(rc // 8, 8, tn).sum(axis=0)
        p1 = (w * s[:, 1:2]).reshape(rc // 8, 8, tn).sum(axis=0)
        return a0 + p0, a1 + p1

    z = jnp.zeros((8, tn), F32)
    a0, a1 = lax.fori_loop(0, d // rc, body, (z, z))
    b = b_ref[...]
    o_ref[0:1, :] = jnp.sum(a0, axis=0, keepdims=True) + b
    o_ref[1:2, :] = jnp.sum(a1, axis=0, keepdims=True) + b


def _ada(c2, w_ada, b_ada):
    d, n = w_ada.shape
    tn = 1024
    return pl.pallas_call(
        functools.partial(_ada_kernel, rc=128),
        out_shape=jax.ShapeDtypeStruct((2, n), F32),
        grid=(n // tn,),
        in_specs=[_const_spec((d, 2)),
                  pl.BlockSpec((d, tn), lambda j: (0, j)),
                  pl.BlockSpec((1, tn), lambda j: (0, j))],
        out_specs=pl.BlockSpec((2, tn), lambda j: (0, j)),
        compiler_params=_cparams(("arbitrary",)),
        name="ada",
    )(c2, w_ada, b_ada.reshape(1, n))


def _head_norm_rope(t, gain, cos, s_up, s_dn):
    r = lax.rsqrt(jnp.sum(t * t, axis=-1, keepdims=True) * (1.0 / QK_DIM) + EPS)
    t = t * r * gain
    return t * cos + pltpu.roll(t, AXIS_ROPE, 1) * s_up + pltpu.roll(t, HEAD_W - AXIS_ROPE, 1) * s_dn


def _inproj_kernel(x_ref, mod_ref, g_ref, win_ref, qa_ref, wq_ref, kva_ref, wkv_ref, gq_ref, gk_ref,
                   qadd_ref, kadd_ref, vone_ref, cos_ref, sup_ref, sdn_ref, glu_ref, q_ref, k_ref, vt_ref):
    x = x_ref[...]
    ms = jnp.mean(x * x, axis=-1, keepdims=True)
    y = x * lax.rsqrt(ms + EPS) * g_ref[...]
    h = y * (1.0 + mod_ref[0, 1:2, :]) + mod_ref[0, 0:1, :]
    proj = jnp.dot(h.astype(BF16), win_ref[...], preferred_element_type=F32)
    glu_ref[...] = proj[:, :CONV_CH] * jax.nn.sigmoid(proj[:, CONV_CH:Q0])

    pq = proj[:, Q0:KV0]
    pq = pq * lax.rsqrt(jnp.mean(pq * pq, axis=-1, keepdims=True) + EPS) * qa_ref[...]
    q = jnp.dot(pq.astype(BF16), wq_ref[...], preferred_element_type=F32)
    pkv = proj[:, KV0:KR0]
    pkv = pkv * lax.rsqrt(jnp.mean(pkv * pkv, axis=-1, keepdims=True) + EPS) * kva_ref[...]
    kv = jnp.dot(pkv.astype(BF16), wkv_ref[...], preferred_element_type=F32)
    kr = proj[:, KR0:KR0 + HEAD_W]

    cos, s_up, s_dn = cos_ref[...], sup_ref[...], sdn_ref[...]
    gq, gk = gq_ref[...], gk_ref[...]
    qadd, kadd = qadd_ref[...], kadd_ref[...]
    for hd in range(N_HEADS):
        sl = slice(hd * HEAD_W, (hd + 1) * HEAD_W)
        q_ref[:, sl] = (_head_norm_rope(q[:, sl], gq, cos, s_up, s_dn) + qadd).astype(BF16)
        k_ref[:, sl] = (_head_norm_rope(kv[:, sl] + kr, gk, cos, s_up, s_dn) + kadd).astype(BF16)
    vt_ref[...] = (kv[:, N_HEADS * HEAD_W:] + vone_ref[...]).T.astype(BF16)


def _inproj(xin, mod1, g_mix, win, qa, wq, kva, wkv, gq, gk, qadd, kadd, vone, cos, s_up, s_dn, n_lat, tm):
    t_all, d = xin.shape
    n_lat_tiles = n_lat // tm
    hw = N_HEADS * HEAD_W
    row = lambda w: pl.BlockSpec((tm, w), lambda i: (i, 0))
    return pl.pallas_call(
        _inproj_kernel,
        out_shape=(jax.ShapeDtypeStruct((t_all, CONV_CH), F32),
                   jax.ShapeDtypeStruct((t_all, hw), BF16),
                   jax.ShapeDtypeStruct((t_all, hw), BF16),
                   jax.ShapeDtypeStruct((hw, t_all), BF16)),
        grid=(t_all // tm,),
        in_specs=[row(d),
                  pl.BlockSpec((1, 2, d), lambda i: (jnp.where(i < n_lat_tiles, 0, 1), 0, 0)),
                  _const_spec(g_mix.shape), _const_spec(win.shape), _const_spec(qa.shape),
                  _const_spec(wq.shape), _const_spec(kva.shape), _const_spec(wkv.shape),
                  _const_spec(gq.shape), _const_spec(gk.shape), _const_spec(qadd.shape),
                  _const_spec(kadd.shape), _const_spec(vone.shape),
                  row(HEAD_W), row(HEAD_W), row(HEAD_W)],
        out_specs=(row(CONV_CH), row(hw), row(hw), pl.BlockSpec((hw, tm), lambda i: (0, i))),
        compiler_params=_cparams(("arbitrary",)),
        name="inproj",
    )(xin, mod1, g_mix, win, qa, wq, kva, wkv, gq, gk, qadd, kadd, vone, cos, s_up, s_dn)


CONV_HALO = 16
CONV_RC = 64


def _conv_kernel(prev_ref, main_ref, next_ref, w_ref, b_ref, g_ref, beta_ref, o_ref, buf_ref):
    i = pl.program_id(0)
    tc = main_ref.shape[0]
    gw = CONV_CH // CONV_GROUPS
    buf_ref[0:CONV_HALO, :] = jnp.where(i > 0, prev_ref[...], 0.0)
    buf_ref[CONV_HALO:CONV_HALO + tc, :] = main_ref[...]
    buf_ref[CONV_HALO + tc:, :] = jnp.where(i < pl.num_programs(0) - 1, next_ref[...], 0.0)
    win = CONV_RC + 2 * CONV_HALO

    def chunk(c, carry):
        r0 = pl.multiple_of(c * CONV_RC, CONV_RC)
        for g in range(CONV_GROUPS):
            ls = slice(g * gw, (g + 1) * gw)
            w_all = buf_ref[pl.ds(r0, win), ls]
            acc = jnp.zeros((CONV_RC, gw), F32)
            for b in range(8):
                wb = w_all[b:b + CONV_RC + 24, :]
                for a in range(4):
                    k = 8 * a + b - 1
                    if 0 <= k < CONV_WIDTH:
                        acc = acc + wb[8 * a:8 * a + CONV_RC, :] * w_ref[k:k + 1, ls]
            acc = acc + b_ref[:, ls]
            mu = jnp.mean(acc, axis=-1, keepdims=True)
            cen = acc - mu
            var = jnp.mean(cen * cen, axis=-1, keepdims=True)
            yv = cen * lax.rsqrt(var + EPS) * g_ref[:, ls] + beta_ref[:, ls]
            o_ref[pl.ds(r0, CONV_RC), ls] = (yv * jax.nn.sigmoid(yv)).astype(o_ref.dtype)
        return carry

    lax.fori_loop(0, tc // CONV_RC, chunk, 0)


def _conv(glu, w, b, g, beta, n_lat, tc):
    hb = tc // CONV_HALO
    return pl.pallas_call(
        _conv_kernel,
        out_shape=jax.ShapeDtypeStruct((n_lat, CONV_CH), BF16),
        grid=(n_lat // tc,),
        in_specs=[pl.BlockSpec((CONV_HALO, CONV_CH), lambda i: (jnp.maximum(i * hb - 1, 0), 0)),
                  pl.BlockSpec((tc, CONV_CH), lambda i: (i, 0)),
                  pl.BlockSpec((CONV_HALO, CONV_CH), lambda i: ((i + 1) * hb, 0)),
                  _const_spec(w.shape), _const_spec(b.shape), _const_spec(g.shape),
                  _const_spec(beta.shape)],
        out_specs=pl.BlockSpec((tc, CONV_CH), lambda i: (i, 0)),
        scratch_shapes=[pltpu.VMEM((tc + 2 * CONV_HALO, CONV_CH), F32)],
        compiler_params=_cparams(("arbitrary",)),
        name="conv",
    )(glu, glu, glu, w, b, g, beta)


ATTN_HEADS_PER_STEP = 2
SHIFT_LIMIT = 40.0


def _attn_fixed_kernel(q_ref, k_ref, vt_ref, o_ref, *, tk):
    tq = q_ref.shape[0]
    n_kv = k_ref.shape[0] // tk
    qs = [q_ref[:, hh * HEAD_W:(hh + 1) * HEAD_W] for hh in range(ATTN_HEADS_PER_STEP)]

    def step(j, carry):
        r0 = pl.multiple_of(j * tk, tk)
        out = []
        for hh in range(ATTN_HEADS_PER_STEP):
            kc = k_ref[pl.ds(r0, tk), hh * HEAD_W:(hh + 1) * HEAD_W]
            vtc = vt_ref[hh * HEAD_W:(hh + 1) * HEAD_W, pl.ds(r0, tk)]
            st = lax.dot_general(kc, qs[hh], (((1,), (1,)), ((), ())), preferred_element_type=F32)
            pt = jnp.exp2(st).astype(BF16)
            out.append(carry[hh] + jnp.dot(vtc, pt, preferred_element_type=F32))
        return tuple(out)

    init = tuple(jnp.zeros((HEAD_W, tq), F32) for _ in range(ATTN_HEADS_PER_STEP))
    res = lax.fori_loop(0, n_kv, step, init)
    outs = [(acc[:V_DIM, :] / acc[V_DIM:V_DIM + 1, :]).T for acc in res]
    o_ref[...] = jnp.concatenate(outs, axis=-1).astype(o_ref.dtype)


def _attn_online_kernel(q_ref, k_ref, vt_ref, o_ref, *, tk):
    tq = q_ref.shape[0]
    n_kv = k_ref.shape[0] // tk
    qs = [q_ref[:, hh * HEAD_W:(hh + 1) * HEAD_W] for hh in range(ATTN_HEADS_PER_STEP)]

    def step(j, carry):
        r0 = pl.multiple_of(j * tk, tk)
        out = []
        for hh in range(ATTN_HEADS_PER_STEP):
            m, acc = carry[hh]
            kc = k_ref[pl.ds(r0, tk), hh * HEAD_W:(hh + 1) * HEAD_W]
            vtc = vt_ref[hh * HEAD_W:(hh + 1) * HEAD_W, pl.ds(r0, tk)]
            st = lax.dot_general(kc, qs[hh], (((1,), (1,)), ((), ())), preferred_element_type=F32)
            m_new = jnp.maximum(m, jnp.max(st, axis=0, keepdims=True))
            pt = jnp.exp2(st - m_new).astype(BF16)
            acc = acc * jnp.exp2(m - m_new) + jnp.dot(vtc, pt, preferred_element_type=F32)
            out.append((m_new, acc))
        return tuple(out)

    init = tuple((jnp.full((1, tq), -jnp.inf, F32), jnp.zeros((HEAD_W, tq), F32))
                 for _ in range(ATTN_HEADS_PER_STEP))
    res = lax.fori_loop(0, n_kv, step, init)
    outs = [(acc[:V_DIM, :] / acc[V_DIM:V_DIM + 1, :]).T for _, acc in res]
    o_ref[...] = jnp.concatenate(outs, axis=-1).astype(o_ref.dtype)


def _attn(body, q, k, vt, n_lat, tq, tk):
    t_all = k.shape[0]
    pw = ATTN_HEADS_PER_STEP * HEAD_W
    return pl.pallas_call(
        functools.partial(body, tk=tk),
        out_shape=jax.ShapeDtypeStruct((n_lat, N_HEADS * V_DIM), BF16),
        grid=(N_HEADS // ATTN_HEADS_PER_STEP, n_lat // tq),
        in_specs=[pl.BlockSpec((tq, pw), lambda hp, i: (i, hp)),
                  pl.BlockSpec((t_all, pw), lambda hp, i: (0, hp)),
                  pl.BlockSpec((pw, t_all), lambda hp, i: (hp, 0))],
        out_specs=pl.BlockSpec((tq, ATTN_HEADS_PER_STEP * V_DIM), lambda hp, i: (i, hp)),
        compiler_params=_cparams(("arbitrary", "arbitrary")),
        name="attn",
    )(q, k, vt)


def _outproj_kernel(conv_ref, attn_ref, x_ref, wo_ref, mod_ref, g_ref, rw_ref, rb_ref,
                    x1_ref, h2_ref, meta_ref, gate_ref, cnt_ref, cnt_acc):
    i = pl.program_id(0)
    tm = x_ref.shape[0]

    @pl.when(i == 0)
    def _():
        cnt_acc[...] = jnp.zeros_like(cnt_acc)

    y = jnp.dot(conv_ref[...], wo_ref[:CONV_CH, :], preferred_element_type=F32)
    y = y + jnp.dot(attn_ref[...], wo_ref[CONV_CH:, :], preferred_element_type=F32)
    x1 = x_ref[...] + mod_ref[0:1, :] * y
    x1_ref[...] = x1
    ms = jnp.mean(x1 * x1, axis=-1, keepdims=True)
    h2 = x1 * lax.rsqrt(ms + EPS) * g_ref[...]
    h2 = h2 * (1.0 + mod_ref[2:3, :]) + mod_ref[1:2, :]
    h2_ref[...] = h2

    h_hi = h2.astype(BF16)
    h_lo = (h2 - h_hi.astype(F32)).astype(BF16)
    rw = rw_ref[...]
    r_hi = rw.astype(BF16)
    r_lo = (rw - r_hi.astype(F32)).astype(BF16)
    logits = (jnp.dot(h_hi, r_hi, preferred_element_type=F32)
              + jnp.dot(h_lo, r_hi, preferred_element_type=F32)
              + jnp.dot(h_hi, r_lo, preferred_element_type=F32)) + rb_ref[...]

    lane = lax.broadcasted_iota(jnp.int32, (tm, LANES), 1)
    work = logits
    vals, idxs, sels = [], [], []
    for _ in range(TOP_K):
        mx = jnp.max(work, axis=-1, keepdims=True)
        idx = jnp.min(jnp.where(work == mx, lane, LANES), axis=-1, keepdims=True)
        sel = lane == idx
        vals.append(mx)
        idxs.append(idx)
        sels.append(sel)
        work = jnp.where(sel, -jnp.inf, work)
    es = [jnp.exp(v - vals[0]) for v in vals]
    denom = es[0] + es[1] + es[2] + es[3]

    chosen = jnp.zeros((tm, LANES), F32)
    for sel in sels:
        chosen = chosen + jnp.where(sel, 1.0, 0.0)
    rr = lax.broadcasted_iota(jnp.int32, (tm, tm), 0)
    cc = lax.broadcasted_iota(jnp.int32, (tm, tm), 1)
    tri = jnp.where(cc < rr, 1.0, 0.0).astype(BF16)
    before = jnp.dot(tri, chosen.astype(BF16), preferred_element_type=F32) + cnt_acc[...]

    meta = jnp.zeros((tm, LANES), jnp.int32)
    gates = jnp.zeros((tm, LANES), F32)
    for kk in range(TOP_K):
        pos = jnp.sum(jnp.where(sels[kk], before, 0.0), axis=-1, keepdims=True).astype(jnp.int32)
        meta = jnp.where(lane == kk, idxs[kk], meta)
        meta = jnp.where(lane == TOP_K + kk, pos, meta)
        gates = jnp.where(lane == kk, es[kk] / denom, gates)
    meta_ref[...] = meta
    gate_ref[...] = gates
    cnt_acc[...] = cnt_acc[...] + jnp.sum(chosen, axis=0, keepdims=True)
    cnt_ref[...] = cnt_acc[...]


def _outproj(conv, attn, x, wo, mod2, g_ffn, rw, rb, tm):
    n_lat, d = x.shape
    row = lambda w: pl.BlockSpec((tm, w), lambda i: (i, 0))
    return pl.pallas_call(
        _outproj_kernel,
        out_shape=(jax.ShapeDtypeStruct((n_lat, d), F32),
                   jax.ShapeDtypeStruct((n_lat, d), F32),
                   jax.ShapeDtypeStruct((n_lat, LANES), jnp.int32),
                   jax.ShapeDtypeStruct((n_lat, LANES), F32),
                   jax.ShapeDtypeStruct((1, LANES), F32)),
        grid=(n_lat // tm,),
        in_specs=[row(CONV_CH), row(N_HEADS * V_DIM), row(d), _const_spec(wo.shape),
                  _const_spec(mod2.shape), _const_spec(g_ffn.shape), _const_spec(rw.shape),
                  _const_spec(rb.shape)],
        out_specs=(row(d), row(d), row(LANES), row(LANES), _const_spec((1, LANES))),
        scratch_shapes=[pltpu.VMEM((1, LANES), F32)],
        compiler_params=_cparams(("arbitrary",)),
        name="outproj",
    )(conv, attn, x, wo, mod2, g_ffn, rw, rb)


ROW_DMA_UNROLL = 8


def _fetch_rows(idx_first_ref, idx_next_ref, src_hbm, buf, sem):
    i = pl.program_id(0)
    n = pl.num_programs(0)
    n_rows = buf.shape[1]
    slot = lax.rem(i, 2)

    def row_copy(idx_ref, sl, r):
        return pltpu.make_async_copy(src_hbm.at[pl.ds(idx_ref[0, 0, r], 1)], buf.at[sl, pl.ds(r, 1)],
                                     sem.at[sl])

    def request(idx_ref, sl):
        def body(r, c):
            row_copy(idx_ref, sl, r).start()
            return c
        lax.fori_loop(0, n_rows, body, 0, unroll=ROW_DMA_UNROLL)

    @pl.when(i == 0)
    def _():
        request(idx_first_ref, 0)

    @pl.when(i + 1 < n)
    def _():
        request(idx_next_ref, 1 - slot)

    def arrive(r, c):
        row_copy(idx_first_ref, slot, r).wait()
        return c
    lax.fori_loop(0, n_rows, arrive, 0, unroll=ROW_DMA_UNROLL)
    return slot


def _idx_specs(n_per_step, n_steps):
    first = pl.BlockSpec((1, 1, n_per_step), lambda i: (0, 0, 0), memory_space=pltpu.SMEM)
    nxt = pl.BlockSpec((1, 1, n_per_step), lambda i: (jnp.minimum(i + 1, n_steps - 1), 0, 0),
                       memory_space=pltpu.SMEM)
    return first, nxt


def _gather_kernel(tok_first_ref, tok_next_ref, h_hbm, o_ref, buf, sem):
    slot = _fetch_rows(tok_first_ref, tok_next_ref, h_hbm, buf, sem)
    o_ref[...] = buf[slot].astype(o_ref.dtype)


def _gather_rows(row_tok, h2, nb):
    n_rows = row_tok.shape[0]
    d = h2.shape[1]
    n_steps = n_rows // nb
    toks = row_tok.reshape(n_steps, 1, nb)
    first, nxt = _idx_specs(nb, n_steps)
    return pl.pallas_call(
        _gather_kernel,
        out_shape=jax.ShapeDtypeStruct((n_rows, d), BF16),
        grid=(n_steps,),
        in_specs=[first, nxt, pl.BlockSpec(memory_space=pl.ANY)],
        out_specs=pl.BlockSpec((nb, d), lambda i: (i, 0)),
        scratch_shapes=[pltpu.VMEM((2, nb, d), F32), pltpu.SemaphoreType.DMA((2,))],
        compiler_params=_cparams(("arbitrary",)),
        name="gather",
    )(toks, toks, h2)


SB_BLOCKS = 12
EXPERT_TF = 256
EXPERT_GROUP = 4


def _expert_kernel(sbe_ref, sbb0_ref, sbnb_ref, nsup_ref, nused_ref,
                   xs_hbm, wg_ref, wl_ref, wd_ref, bg_ref, bl_ref, bd_ref, ys_hbm,
                   xbuf, acc, wgl_s, wd_s, zbuf, xsem, ysem):
    s = pl.program_id(0)
    f = pl.program_id(1)
    nf = pl.num_programs(1)
    tf = wd_s.shape[0]
    n_sup = nsup_ref[0]
    n_blocks = ys_hbm.shape[0] // ROW_BLK

    def x_copy(sb, slot, b):
        row0 = pl.multiple_of((sbb0_ref[sb] + b) * ROW_BLK, ROW_BLK)
        return pltpu.make_async_copy(xs_hbm.at[pl.ds(row0, ROW_BLK)],
                                     xbuf.at[slot, pl.ds(pl.multiple_of(b * ROW_BLK, ROW_BLK), ROW_BLK)],
                                     xsem.at[slot])

    def y_copy(sb, b):
        row0 = pl.multiple_of((sbb0_ref[sb] + b) * ROW_BLK, ROW_BLK)
        return pltpu.make_async_copy(acc.at[pl.ds(pl.multiple_of(b * ROW_BLK, ROW_BLK), ROW_BLK)],
                                     ys_hbm.at[pl.ds(row0, ROW_BLK)], ysem)

    def z_copy(b):
        return pltpu.make_async_copy(zbuf, ys_hbm.at[pl.ds(pl.multiple_of(b * ROW_BLK, ROW_BLK), ROW_BLK)],
                                     ysem)

    def for_blocks(n, fn):
        def body(b, c):
            fn(b)
            return c
        lax.fori_loop(0, n, body, 0)

    @pl.when(s < n_sup)
    def _():
        slot = lax.rem(s, 2)
        nb = sbnb_ref[s]

        @pl.when(f == 0)
        def _():
            @pl.when(s == 0)
            def _():
                for_blocks(nb, lambda b: x_copy(0, 0, b).start())

            for_blocks(nb, lambda b: x_copy(s, slot, b).wait())

            @pl.when(s + 1 < n_sup)
            def _():
                nxt = jnp.minimum(s + 1, n_sup - 1)
                for_blocks(sbnb_ref[nxt], lambda b: x_copy(nxt, 1 - slot, b).start())

        wgl_s[:, :tf] = wg_ref[0].astype(BF16)
        wgl_s[:, tf:] = wl_ref[0].astype(BF16)
        wd_s[...] = wd_ref[0].astype(BF16)
        bg, bl = bg_ref[0], bl_ref[0]

        @pl.when(f == 0)
        def _():
            bias = jnp.broadcast_to(bd_ref[0], (ROW_BLK, acc.shape[1]))

            def init(b):
                acc[pl.ds(pl.multiple_of(b * ROW_BLK, ROW_BLK), ROW_BLK), :] = bias
            for_blocks(nb, init)

        def add_tile(b, m):
            rows = m * ROW_BLK
            r0 = pl.multiple_of(b * ROW_BLK, ROW_BLK)
            gu = jnp.dot(xbuf[slot, pl.ds(r0, rows), :], wgl_s[...], preferred_element_type=F32)
            glu = jnp.minimum(gu[:, :tf] + bg, SWIGLU_LIMIT)
            lin = jnp.clip(gu[:, tf:] + bl, -SWIGLU_LIMIT, SWIGLU_LIMIT)
            hmid = glu * jax.nn.sigmoid(SWIGLU_ALPHA * glu) * (lin + 1.0)
            acc[pl.ds(r0, rows), :] += jnp.dot(hmid.astype(BF16), wd_s[...], preferred_element_type=F32)

            @pl.when(f == nf - 1)
            def _():
                for u in range(m):
                    y_copy(s, b + u).start()

        n_grp = nb // EXPERT_GROUP
        for_blocks(n_grp, lambda g: add_tile(g * EXPERT_GROUP, EXPERT_GROUP))
        done = n_grp * EXPERT_GROUP
        m = EXPERT_GROUP // 2
        while m >= 1:
            take = ((nb - done) // m) > 0

            @pl.when(take)
            def _(m=m, done=done):
                add_tile(done, m)
            done = done + jnp.where(take, m, 0)
            m //= 2

        @pl.when(f == nf - 1)
        def _():
            for_blocks(nb, lambda b: y_copy(s, b).wait())

    @pl.when((s == pl.num_programs(0) - 1) & (f == nf - 1))
    def _():
        zbuf[...] = jnp.zeros_like(zbuf)
        n_used = nused_ref[0]

        def tail(fn):
            def body(b, c):
                fn(b)
                return c
            lax.fori_loop(n_used, n_blocks, body, 0)

        tail(lambda b: z_copy(b).start())
        tail(lambda b: z_copy(b).wait())


def _experts(sb_e, sb_b0, sb_nb, n_sup, n_used, xs, w_gate_up, b_gate_up, w_down, b_down):
    n_rows, d = xs.shape
    tf = EXPERT_TF
    de = w_down.shape[1]
    nf = de // tf
    n_e = w_down.shape[0]
    n_sb = sb_e.shape[0]
    rs = SB_BLOCKS * ROW_BLK
    bgu = b_gate_up.reshape(n_e, 1, 2 * de)
    bdn = b_down.reshape(n_e, 1, d)

    def e_of(s, nsup, sbe):
        return sbe[jnp.minimum(s, nsup[0] - 1)]

    def f_of(s, f, nsup):
        return jnp.where(s < nsup[0], f, nf - 1)

    grid_spec = pltpu.PrefetchScalarGridSpec(
        num_scalar_prefetch=5,
        grid=(n_sb, nf),
        in_specs=[
            pl.BlockSpec(memory_space=pl.ANY),
            pl.BlockSpec((1, d, tf), lambda s, f, sbe, b0, nb, ns, nu: (e_of(s, ns, sbe), 0, f_of(s, f, ns))),
            pl.BlockSpec((1, d, tf), lambda s, f, sbe, b0, nb, ns, nu: (e_of(s, ns, sbe), 0, nf + f_of(s, f, ns))),
            pl.BlockSpec((1, tf, d), lambda s, f, sbe, b0, nb, ns, nu: (e_of(s, ns, sbe), f_of(s, f, ns), 0)),
            pl.BlockSpec((1, 1, tf), lambda s, f, sbe, b0, nb, ns, nu: (e_of(s, ns, sbe), 0, f_of(s, f, ns))),
            pl.BlockSpec((1, 1, tf), lambda s, f, sbe, b0, nb, ns, nu: (e_of(s, ns, sbe), 0, nf + f_of(s, f, ns))),
            pl.BlockSpec((1, 1, d), lambda s, f, sbe, b0, nb, ns, nu: (e_of(s, ns, sbe), 0, 0)),
        ],
        out_specs=pl.BlockSpec(memory_space=pl.ANY),
        scratch_shapes=[pltpu.VMEM((2, rs, d), BF16), pltpu.VMEM((rs, d), F32),
                        pltpu.VMEM((d, 2 * tf), BF16), pltpu.VMEM((tf, d), BF16),
                        pltpu.VMEM((ROW_BLK, d), F32),
                        pltpu.SemaphoreType.DMA((2,)), pltpu.SemaphoreType.DMA],
    )
    return pl.pallas_call(
        _expert_kernel,
        out_shape=jax.ShapeDtypeStruct((n_rows, d), F32),
        grid_spec=grid_spec,
        compiler_params=_cparams(("arbitrary", "arbitrary")),
        name="experts",
    )(sb_e, sb_b0, sb_nb, n_sup, n_used, xs, w_gate_up, w_gate_up, w_down, bgu, bgu, bdn)


def _combine_kernel(dest_first_ref, dest_next_ref, y_hbm, x1_ref, gate_ref, g2_ref, o_ref, buf, sem):
    tm = x1_ref.shape[0]
    slot = _fetch_rows(dest_first_ref, dest_next_ref, y_hbm, buf, sem)
    gates = gate_ref[...]
    acc = buf[slot, 0:tm, :] * gates[:, 0:1]
    for kk in range(1, TOP_K):
        acc = acc + buf[slot, kk * tm:(kk + 1) * tm, :] * gates[:, kk:kk + 1]
    o_ref[...] = x1_ref[...] + g2_ref[...] * acc


def _combine(dest_km, yb, x1, gates, g2, tm):
    n_lat, d = x1.shape
    n_steps = n_lat // tm
    first, nxt = _idx_specs(TOP_K * tm, n_steps)
    return pl.pallas_call(
        _combine_kernel,
        out_shape=jax.ShapeDtypeStruct((n_lat, d), F32),
        grid=(n_steps,),
        in_specs=[first, nxt,
                  pl.BlockSpec(memory_space=pl.ANY),
                  pl.BlockSpec((tm, d), lambda i: (i, 0)),
                  pl.BlockSpec((tm, LANES), lambda i: (i, 0)),
                  _const_spec((1, d))],
        out_specs=pl.BlockSpec((tm, d), lambda i: (i, 0)),
        scratch_shapes=[pltpu.VMEM((2, TOP_K * tm, d), F32), pltpu.SemaphoreType.DMA((2,))],
        compiler_params=_cparams(("arbitrary",)),
        name="combine",
    )(dest_km, dest_km, yb, x1, gates, g2)


def _head_cols(w, per_head, start, width, perm=None):
    k = w.shape[0]
    t = w.reshape(k, N_HEADS, per_head)[:, :, start:start + width]
    return t if perm is None else t[:, :, jnp.asarray(perm)]


def _head_layout(nope, rope):
    k = nope.shape[0]
    parts = [nope]
    if rope is not None:
        parts.append(rope)
    used = sum(p.shape[-1] for p in parts)
    parts.append(jnp.zeros((k, N_HEADS, HEAD_W - used), nope.dtype))
    return jnp.concatenate(parts, axis=-1).reshape(k, N_HEADS * HEAD_W)


def _gain_row(g, mult=1.0):
    perm = jnp.asarray(_ROPE_PERM)
    return jnp.concatenate([g[:QK_NOPE], g[QK_NOPE:][perm], jnp.zeros((HEAD_W - QK_DIM,), F32)])[None] * mult


def _rope_tables(n_lat, n_ctx):
    rows = n_lat // GRID_W
    row = jnp.broadcast_to(jnp.arange(rows, dtype=F32)[:, None], (rows, GRID_W)).reshape(-1)
    col = jnp.broadcast_to(jnp.arange(GRID_W, dtype=F32)[None, :], (rows, GRID_W)).reshape(-1)
    inv_freq = ROPE_THETA ** (-2.0 * jnp.arange(AXIS_PAIRS, dtype=F32) / AXIS_ROPE)
    ang = jnp.concatenate([row[:, None] * inv_freq, col[:, None] * inv_freq], axis=1)
    cs, sn = jnp.cos(ang), jnp.sin(ang)
    one = jnp.ones((n_lat, QK_NOPE), F32)
    zn = jnp.zeros((n_lat, QK_NOPE), F32)
    zp = jnp.zeros((n_lat, HEAD_W - QK_DIM), F32)
    za = jnp.zeros((n_lat, AXIS_ROPE), F32)
    cos = jnp.concatenate([one, cs, cs, zp], axis=1)
    s_up = jnp.concatenate([zn, za, sn, zp], axis=1)
    s_dn = jnp.concatenate([zn, -sn, za, zp], axis=1)
    pad = lambda t, fill: jnp.concatenate([t, jnp.full((n_ctx, HEAD_W), fill, F32)], axis=0)
    return pad(cos, 1.0), pad(s_up, 0.0), pad(s_dn, 0.0)


def kernel(x, c, ctx, c_ctx, w_ada, b_ada, norm_mix, norm_ffn, w_in, conv_w, conv_b, conv_norm_g, conv_norm_b, q_a_norm, w_q_up, kv_a_norm, w_kv_up, q_head_norm, k_head_norm, w_out, router_w, router_b, w_gate_up, b_gate_up, w_down, b_down):
    bsz, n_lat, d = x.shape
    n_ctx = ctx.shape[1]
    assert bsz == 1 and w_ada.shape[0] == 1
    n_e = router_w.shape[-1]
    perm = _ROPE_PERM

    mod = _ada(jnp.stack([c[0], c_ctx], axis=1), w_ada[0], b_ada[0])
    mod = mod.reshape(2, N_MOD, d)
    mod1 = mod[:, 0:2, :]
    mod2 = jnp.stack([mod[0, 2], mod[0, 3], mod[0, 4]], axis=0)
    g2 = mod[0, 5][None]

    wi = w_in[0]
    kr_cols = jnp.concatenate([jnp.zeros((d, QK_NOPE), F32), wi[:, KR0:][:, jnp.asarray(perm)],
                               jnp.zeros((d, HEAD_W - QK_DIM), F32)], axis=1)
    win = jnp.concatenate([wi[:, :KR0], kr_cols], axis=1).astype(BF16)
    wq = _head_layout(_head_cols(w_q_up[0], QK_DIM, 0, QK_NOPE),
                      _head_cols(w_q_up[0], QK_DIM, QK_NOPE, QK_ROPE, perm)).astype(BF16)
    wk = _head_layout(_head_cols(w_kv_up[0], QK_NOPE + V_DIM, 0, QK_NOPE), None)
    wv = _head_layout(_head_cols(w_kv_up[0], QK_NOPE + V_DIM, QK_NOPE, V_DIM), None)
    wkv = jnp.concatenate([wk, wv], axis=1).astype(BF16)
    gq = _gain_row(q_head_norm[0], SCALE * math.log2(math.e))
    gk = _gain_row(k_head_norm[0])
    vone = jnp.zeros((N_HEADS, HEAD_W), F32).at[:, V_DIM].set(1.0).reshape(1, N_HEADS * HEAD_W)
    cos, s_up, s_dn = _rope_tables(n_lat, n_ctx)

    bound = (QK_DIM * SCALE * math.log2(math.e) * 1.01) * (jnp.max(jnp.abs(q_head_norm[0]))
                                                          * jnp.max(jnp.abs(k_head_norm[0])))
    shift = jnp.ceil(bound * 4.0) * 0.25
    fixed_ok = shift <= SHIFT_LIMIT
    lane = jnp.arange(HEAD_W)[None]
    qadd = jnp.where(lane == QK_DIM, jnp.where(fixed_ok, -shift, 0.0), 0.0).astype(F32)
    kadd = jnp.where(lane == QK_DIM, 1.0, 0.0).astype(F32)

    xin = jnp.concatenate([x[0], ctx[0]], axis=0)
    glu, q, k, vt = _inproj(xin, mod1, norm_mix, win, q_a_norm, wq, kv_a_norm, wkv, gq, gk, qadd, kadd, vone,
                            cos, s_up, s_dn, n_lat, tm=256)
    conv = _conv(glu, conv_w[0], conv_b, conv_norm_g, conv_norm_b, n_lat, tc=512)
    t_all = n_lat + n_ctx
    tk = max(t for t in range(256, 1025, 256) if t_all % t == 0)
    tq = min(2048, n_lat)
    attn = lax.cond(fixed_ok,
                    lambda: _attn(_attn_fixed_kernel, q, k, vt, n_lat, tq, tk),
                    lambda: _attn(_attn_online_kernel, q, k, vt, n_lat, tq, tk))

    rw = jnp.concatenate([router_w[0], jnp.zeros((d, LANES - n_e), F32)], axis=1)
    rb = jnp.concatenate([router_b[0], jnp.full((LANES - n_e,), -1e30, F32)])[None]
    x1, h2, meta, gates, cnt = _outproj(conv, attn, x[0], w_out[0].astype(BF16), mod2, norm_ffn, rw, rb,
                                        tm=256)

    out = _moe(x1, h2, meta, gates, cnt, g2, w_gate_up[0], b_gate_up[0], w_down[0], b_down[0])
    return out[None]


def _moe(x1, h2, meta, gates, cnt, g2, w_gate_up, b_gate_up, w_down, b_down):
    n_lat, d = h2.shape
    n_e = w_down.shape[0]
    counts = cnt[0, :n_e].astype(jnp.int32)
    top_idx = meta[:, :TOP_K]
    pos = meta[:, TOP_K:2 * TOP_K]
    nblk_e = (counts + ROW_BLK - 1) // ROW_BLK
    blk_end = jnp.cumsum(nblk_e)
    pad_start = (blk_end - nblk_e) * ROW_BLK
    dest = pad_start[top_idx] + pos
    n_blocks = (n_lat * TOP_K) // ROW_BLK + n_e
    n_rows = n_blocks * ROW_BLK
    tok = jnp.broadcast_to(jnp.arange(n_lat, dtype=jnp.int32)[:, None], (n_lat, TOP_K))
    row_tok = jnp.zeros((n_rows,), jnp.int32).at[dest.reshape(-1)].set(tok.reshape(-1))
    n_used = blk_end[-1:].astype(jnp.int32)
    nsb_e = (nblk_e + SB_BLOCKS - 1) // SB_BLOCKS
    sb_end = jnp.cumsum(nsb_e)
    n_sb = -(-n_blocks // SB_BLOCKS) + n_e
    j = jnp.arange(n_sb)
    sb_e = jnp.minimum(jnp.sum(sb_end[None, :] <= j[:, None], axis=1), n_e - 1).astype(jnp.int32)
    local = j - (sb_end - nsb_e)[sb_e]
    sb_b0 = ((blk_end - nblk_e)[sb_e] + local * SB_BLOCKS).astype(jnp.int32)
    sb_nb = jnp.clip(nblk_e[sb_e] - local * SB_BLOCKS, 0, SB_BLOCKS).astype(jnp.int32)
    n_sup = sb_end[-1:].astype(jnp.int32)
    sb_b0 = jnp.where(j < n_sup[0], sb_b0, 0)
    sb_nb = jnp.where(j < n_sup[0], sb_nb, 0)

    xb = _gather_rows(row_tok, h2, nb=512)
    yb = _experts(sb_e, sb_b0, sb_nb, n_sup, n_used, xb, w_gate_up, b_gate_up, w_down, b_down)

    tm_c = 128
    dest_km = dest.reshape(n_lat // tm_c, tm_c, TOP_K).transpose(0, 2, 1).reshape(n_lat // tm_c, 1,
                                                                                  TOP_K * tm_c)
    return _combine(dest_km, yb, x1, gates, g2, tm=tm_c)
```

```python
import functools
import math

import jax
import jax.numpy as jnp
from jax import lax
from jax.experimental import pallas as pl
from jax.experimental.pallas import tpu as pltpu

F32 = jnp.float32
BF16 = jnp.bfloat16

EPS = 1e-6
N_MOD = 6
GRID_W = 64
CONV_CH = 1024
CONV_GROUPS = 8
CONV_WIDTH = 31
N_HEADS = 16
Q_LORA = 512
KV_LORA = 256
QK_NOPE = 64
QK_ROPE = 32
V_DIM = 64
QK_DIM = QK_NOPE + QK_ROPE
SCALE = QK_DIM ** -0.5
ROPE_THETA = 10000.0
AXIS_ROPE = QK_ROPE // 2
AXIS_PAIRS = AXIS_ROPE // 2
Q0 = 2 * CONV_CH
KV0 = Q0 + Q_LORA
KR0 = KV0 + KV_LORA
N_EXPERTS = 32
TOP_K = 4
SWIGLU_LIMIT = 7.0
SWIGLU_ALPHA = 1.702

LANES = 128
HEAD_W = LANES
ROW_BLK = 128
VMEM_LIMIT = 56 * 1024 * 1024

_ROPE_PERM = tuple(ax * AXIS_ROPE + ab * AXIS_PAIRS + p
                   for ab in range(2) for ax in range(2) for p in range(AXIS_PAIRS))


def _cparams(sem, limit=VMEM_LIMIT):
    return pltpu.CompilerParams(dimension_semantics=sem, vmem_limit_bytes=limit)


def _const_spec(shape):
    nd = len(shape)
    return pl.BlockSpec(shape, lambda *_: (0,) * nd)


def _ada_kernel(c_ref, w_ref, b_ref, o_ref, *, rc):
    d, tn = w_ref.shape

    def body(k, carry):
        a0, a1 = carry
        r0 = pl.multiple_of(k * rc, rc)
        cc = c_ref[pl.ds(r0, rc), :]
        s = cc * jax.nn.sigmoid(cc)
        w = w_ref[pl.ds(r0, rc), :]
        p0 = (w * s[:, 0:1]).reshape(rc // 8, 8, tn).sum(axis=0)
        p1 = (w * s[:, 1:2]).reshape(rc // 8, 8, tn).sum(axis=0)
        return a0 + p0, a1 + p1

    z = jnp.zeros((8, tn), F32)
    a0, a1 = lax.fori_loop(0, d // rc, body, (z, z))
    b = b_ref[...]
    o_ref[0:1, :] = jnp.sum(a0, axis=0, keepdims=True) + b
    o_ref[1:2, :] = jnp.sum(a1, axis=0, keepdims=True) + b


def _ada(c2, w_ada, b_ada):
    d, n = w_ada.shape
    tn = 1024
    return pl.pallas_call(
        functools.partial(_ada_kernel, rc=128),
        out_shape=jax.ShapeDtypeStruct((2, n), F32),
        grid=(n // tn,),
        in_specs=[_const_spec((d, 2)),
                  pl.BlockSpec((d, tn), lambda j: (0, j)),
                  pl.BlockSpec((1, tn), lambda j: (0, j))],
        out_specs=pl.BlockSpec((2, tn), lambda j: (0, j)),
        compiler_params=_cparams(("arbitrary",)),
        name="ada",
    )(c2, w_ada, b_ada.reshape(1, n))


def _head_norm_rope(t, gain, cos, s_up, s_dn):
    r = lax.rsqrt(jnp.sum(t * t, axis=-1, keepdims=True) * (1.0 / QK_DIM) + EPS)
    t = t * r * gain
    return t * cos + pltpu.roll(t, AXIS_ROPE, 1) * s_up + pltpu.roll(t, HEAD_W - AXIS_ROPE, 1) * s_dn


def _inproj_kernel(x_ref, mod_ref, g_ref, win_ref, qa_ref, wq_ref, kva_ref, wkv_ref, gq_ref, gk_ref,
                   qadd_ref, kadd_ref, vone_ref, cos_ref, sup_ref, sdn_ref, glu_ref, q_ref, k_ref, vt_ref):
    x = x_ref[...]
    ms = jnp.mean(x * x, axis=-1, keepdims=True)
    y = x * lax.rsqrt(ms + EPS) * g_ref[...]
    h = y * (1.0 + mod_ref[0, 1:2, :]) + mod_ref[0, 0:1, :]
    proj = jnp.dot(h.astype(BF16), win_ref[...], preferred_element_type=F32)
    glu_ref[...] = proj[:, :CONV_CH] * jax.nn.sigmoid(proj[:, CONV_CH:Q0])

    pq = proj[:, Q0:KV0]
    pq = pq * lax.rsqrt(jnp.mean(pq * pq, axis=-1, keepdims=True) + EPS) * qa_ref[...]
    q = jnp.dot(pq.astype(BF16), wq_ref[...], preferred_element_type=F32)
    pkv = proj[:, KV0:KR0]
    pkv = pkv * lax.rsqrt(jnp.mean(pkv * pkv, axis=-1, keepdims=True) + EPS) * kva_ref[...]
    kv = jnp.dot(pkv.astype(BF16), wkv_ref[...], preferred_element_type=F32)
    kr = proj[:, KR0:KR0 + HEAD_W]

    cos, s_up, s_dn = cos_ref[...], sup_ref[...], sdn_ref[...]
    gq, gk = gq_ref[...], gk_ref[...]
    qadd, kadd = qadd_ref[...], kadd_ref[...]
    for hd in range(N_HEADS):
        sl = slice(hd * HEAD_W, (hd + 1) * HEAD_W)
        q_ref[:, sl] = (_head_norm_rope(q[:, sl], gq, cos, s_up, s_dn) + qadd).astype(BF16)
        k_ref[:, sl] = (_head_norm_rope(kv[:, sl] + kr, gk, cos, s_up, s_dn) + kadd).astype(BF16)
    vt_ref[...] = (kv[:, N_HEADS * HEAD_W:] + vone_ref[...]).T.astype(BF16)


def _inproj(xin, mod1, g_mix, win, qa, wq, kva, wkv, gq, gk, qadd, kadd, vone, cos, s_up, s_dn, n_lat, tm):
    t_all, d = xin.shape
    n_lat_tiles = n_lat // tm
    hw = N_HEADS * HEAD_W
    row = lambda w: pl.BlockSpec((tm, w), lambda i: (i, 0))
    return pl.pallas_call(
        _inproj_kernel,
        out_shape=(jax.ShapeDtypeStruct((t_all, CONV_CH), F32),
                   jax.ShapeDtypeStruct((t_all, hw), BF16),
                   jax.ShapeDtypeStruct((t_all, hw), BF16),
                   jax.ShapeDtypeStruct((hw, t_all), BF16)),
        grid=(t_all // tm,),
        in_specs=[row(d),
                  pl.BlockSpec((1, 2, d), lambda i: (jnp.where(i < n_lat_tiles, 0, 1), 0, 0)),
                  _const_spec(g_mix.shape), _const_spec(win.shape), _const_spec(qa.shape),
                  _const_spec(wq.shape), _const_spec(kva.shape), _const_spec(wkv.shape),
                  _const_spec(gq.shape), _const_spec(gk.shape), _const_spec(qadd.shape),
                  _const_spec(kadd.shape), _const_spec(vone.shape),
                  row(HEAD_W), row(HEAD_W), row(HEAD_W)],
        out_specs=(row(CONV_CH), row(hw), row(hw), pl.BlockSpec((hw, tm), lambda i: (0, i))),
        compiler_params=_cparams(("arbitrary",)),
        name="inproj",
    )(xin, mod1, g_mix, win, qa, wq, kva, wkv, gq, gk, qadd, kadd, vone, cos, s_up, s_dn)


CONV_HALO = 16
CONV_RC = 64


def _conv_kernel(prev_ref, main_ref, next_ref, w_ref, b_ref, g_ref, beta_ref, o_ref, buf_ref):
    i = pl.program_id(0)
    tc = main_ref.shape[0]
    gw = CONV_CH // CONV_GROUPS
    buf_ref[0:CONV_HALO, :] = jnp.where(i > 0, prev_ref[...], 0.0)
    buf_ref[CONV_HALO:CONV_HALO + tc, :] = main_ref[...]
    buf_ref[CONV_HALO + tc:, :] = jnp.where(i < pl.num_programs(0) - 1, next_ref[...], 0.0)
    win = CONV_RC + 2 * CONV_HALO

    def chunk(c, carry):
        r0 = pl.multiple_of(c * CONV_RC, CONV_RC)
        for g in range(CONV_GROUPS):
            ls = slice(g * gw, (g + 1) * gw)
            w_all = buf_ref[pl.ds(r0, win), ls]
            acc = jnp.zeros((CONV_RC, gw), F32)
            for b in range(8):
                wb = w_all[b:b + CONV_RC + 24, :]
                for a in range(4):
                    k = 8 * a + b - 1
                    if 0 <= k < CONV_WIDTH:
                        acc = acc + wb[8 * a:8 * a + CONV_RC, :] * w_ref[k:k + 1, ls]
            acc = acc + b_ref[:, ls]
            mu = jnp.mean(acc, axis=-1, keepdims=True)
            cen = acc - mu
            var = jnp.mean(cen * cen, axis=-1, keepdims=True)
            yv = cen * lax.rsqrt(var + EPS) * g_ref[:, ls] + beta_ref[:, ls]
            o_ref[pl.ds(r0, CONV_RC), ls] = (yv * jax.nn.sigmoid(yv)).astype(o_ref.dtype)
        return carry

    lax.fori_loop(0, tc // CONV_RC, chunk, 0)


def _conv(glu, w, b, g, beta, n_lat, tc):
    hb = tc // CONV_HALO
    return pl.pallas_call(
        _conv_kernel,
        out_shape=jax.ShapeDtypeStruct((n_lat, CONV_CH), BF16),
        grid=(n_lat // tc,),
        in_specs=[pl.BlockSpec((CONV_HALO, CONV_CH), lambda i: (jnp.maximum(i * hb - 1, 0), 0)),
                  pl.BlockSpec((tc, CONV_CH), lambda i: (i, 0)),
                  pl.BlockSpec((CONV_HALO, CONV_CH), lambda i: ((i + 1) * hb, 0)),
                  _const_spec(w.shape), _const_spec(b.shape), _const_spec(g.shape),
                  _const_spec(beta.shape)],
        out_specs=pl.BlockSpec((tc, CONV_CH), lambda i: (i, 0)),
        scratch_shapes=[pltpu.VMEM((tc + 2 * CONV_HALO, CONV_CH), F32)],
        compiler_params=_cparams(("arbitrary",)),
        name="conv",
    )(glu, glu, glu, w, b, g, beta)


ATTN_HEADS_PER_STEP = 2
SHIFT_LIMIT = 40.0


def _attn_fixed_kernel(q_ref, k_ref, vt_ref, o_ref, *, tk):
    tq = q_ref.shape[0]
    n_kv = k_ref.shape[0] // tk
    qs = [q_ref[:, hh * HEAD_W:(hh + 1) * HEAD_W] for hh in range(ATTN_HEADS_PER_STEP)]

    def step(j, carry):
        r0 = pl.multiple_of(j * tk, tk)
        out = []
        for hh in range(ATTN_HEADS_PER_STEP):
            kc = k_ref[pl.ds(r0, tk), hh * HEAD_W:(hh + 1) * HEAD_W]
            vtc = vt_ref[hh * HEAD_W:(hh + 1) * HEAD_W, pl.ds(r0, tk)]
            st = lax.dot_general(kc, qs[hh], (((1,), (1,)), ((), ())), preferred_element_type=F32)
            pt = jnp.exp2(st).astype(BF16)
            out.append(carry[hh] + jnp.dot(vtc, pt, preferred_element_type=F32))
        return tuple(out)

    init = tuple(jnp.zeros((HEAD_W, tq), F32) for _ in range(ATTN_HEADS_PER_STEP))
    res = lax.fori_loop(0, n_kv, step, init)
    outs = [(acc[:V_DIM, :] / acc[V_DIM:V_DIM + 1, :]).T for acc in res]
    o_ref[...] = jnp.concatenate(outs, axis=-1).astype(o_ref.dtype)


def _attn_online_kernel(q_ref, k_ref, vt_ref, o_ref, *, tk):
    tq = q_ref.shape[0]
    n_kv = k_ref.shape[0] // tk
    qs = [q_ref[:, hh * HEAD_W:(hh + 1) * HEAD_W] for hh in range(ATTN_HEADS_PER_STEP)]

    def step(j, carry):
        r0 = pl.multiple_of(j * tk, tk)
        out = []
        for hh in range(ATTN_HEADS_PER_STEP):
            m, acc = carry[hh]
            kc = k_ref[pl.ds(r0, tk), hh * HEAD_W:(hh + 1) * HEAD_W]
            vtc = vt_ref[hh * HEAD_W:(hh + 1) * HEAD_W, pl.ds(r0, tk)]
            st = lax.dot_general(kc, qs[hh], (((1,), (1,)), ((), ())), preferred_element_type=F32)
            m_new = jnp.maximum(m, jnp.max(st, axis=0, keepdims=True))
            pt = jnp.exp2(st - m_new).astype(BF16)
            acc = acc * jnp.exp2(m - m_new) + jnp.dot(vtc, pt, preferred_element_type=F32)
            out.append((m_new, acc))
        return tuple(out)

    init = tuple((jnp.full((1, tq), -jnp.inf, F32), jnp.zeros((HEAD_W, tq), F32))
                 for _ in range(ATTN_HEADS_PER_STEP))
    res = lax.fori_loop(0, n_kv, step, init)
    outs = [(acc[:V_DIM, :] / acc[V_DIM:V_DIM + 1, :]).T for _, acc in res]
    o_ref[...] = jnp.concatenate(outs, axis=-1).astype(o_ref.dtype)


def _attn(body, q, k, vt, n_lat, tq, tk):
    t_all = k.shape[0]
    pw = ATTN_HEADS_PER_STEP * HEAD_W
    return pl.pallas_call(
        functools.partial(body, tk=tk),
        out_shape=jax.ShapeDtypeStruct((n_lat, N_HEADS * V_DIM), BF16),
        grid=(N_HEADS // ATTN_HEADS_PER_STEP, n_lat // tq),
        in_specs=[pl.BlockSpec((tq, pw), lambda hp, i: (i, hp)),
                  pl.BlockSpec((t_all, pw), lambda hp, i: (0, hp)),
                  pl.BlockSpec((pw, t_all), lambda hp, i: (hp, 0))],
        out_specs=pl.BlockSpec((tq, ATTN_HEADS_PER_STEP * V_DIM), lambda hp, i: (i, hp)),
        compiler_params=_cparams(("arbitrary", "arbitrary")),
        name="attn",
    )(q, k, vt)


def _outproj_kernel(conv_ref, attn_ref, x_ref, wo_ref, mod_ref, g_ref, rw_ref, rb_ref,
                    x1_ref, h2_ref, meta_ref, gate_ref, cnt_ref, cnt_acc):
    i = pl.program_id(0)
    tm = x_ref.shape[0]

    @pl.when(i == 0)
    def _():
        cnt_acc[...] = jnp.zeros_like(cnt_acc)

    y = jnp.dot(conv_ref[...], wo_ref[:CONV_CH, :], preferred_element_type=F32)
    y = y + jnp.dot(attn_ref[...], wo_ref[CONV_CH:, :], preferred_element_type=F32)
    x1 = x_ref[...] + mod_ref[0:1, :] * y
    x1_ref[...] = x1
    ms = jnp.mean(x1 * x1, axis=-1, keepdims=True)
    h2 = x1 * lax.rsqrt(ms + EPS) * g_ref[...]
    h2 = h2 * (1.0 + mod_ref[2:3, :]) + mod_ref[1:2, :]
    h2_ref[...] = h2

    h_hi = h2.astype(BF16)
    h_lo = (h2 - h_hi.astype(F32)).astype(BF16)
    rw = rw_ref[...]
    r_hi = rw.astype(BF16)
    r_lo = (rw - r_hi.astype(F32)).astype(BF16)
    logits = (jnp.dot(h_hi, r_hi, preferred_element_type=F32)
              + jnp.dot(h_lo, r_hi, preferred_element_type=F32)
              + jnp.dot(h_hi, r_lo, preferred_element_type=F32)) + rb_ref[...]

    lane = lax.broadcasted_iota(jnp.int32, (tm, LANES), 1)
    work = logits
    vals, idxs, sels = [], [], []
    for _ in range(TOP_K):
        mx = jnp.max(work, axis=-1, keepdims=True)
        idx = jnp.min(jnp.where(work == mx, lane, LANES), axis=-1, keepdims=True)
        sel = lane == idx
        vals.append(mx)
        idxs.append(idx)
        sels.append(sel)
        work = jnp.where(sel, -jnp.inf, work)
    es = [jnp.exp(v - vals[0]) for v in vals]
    denom = es[0] + es[1] + es[2] + es[3]

    chosen = jnp.zeros((tm, LANES), F32)
    for sel in sels:
        chosen = chosen + jnp.where(sel, 1.0, 0.0)
    rr = lax.broadcasted_iota(jnp.int32, (tm, tm), 0)
    cc = lax.broadcasted_iota(jnp.int32, (tm, tm), 1)
    tri = jnp.where(cc < rr, 1.0, 0.0).astype(BF16)
    before = jnp.dot(tri, chosen.astype(BF16), preferred_element_type=F32) + cnt_acc[...]

    meta = jnp.zeros((tm, LANES), jnp.int32)
    gates = jnp.zeros((tm, LANES), F32)
    for kk in range(TOP_K):
        pos = jnp.sum(jnp.where(sels[kk], before, 0.0), axis=-1, keepdims=True).astype(jnp.int32)
        meta = jnp.where(lane == kk, idxs[kk], meta)
        meta = jnp.where(lane == TOP_K + kk, pos, meta)
        gates = jnp.where(lane == kk, es[kk] / denom, gates)
    meta_ref[...] = meta
    gate_ref[...] = gates
    cnt_acc[...] = cnt_acc[...] + jnp.sum(chosen, axis=0, keepdims=True)
    cnt_ref[...] = cnt_acc[...]


def _outproj(conv, attn, x, wo, mod2, g_ffn, rw, rb, tm):
    n_lat, d = x.shape
    row = lambda w: pl.BlockSpec((tm, w), lambda i: (i, 0))
    return pl.pallas_call(
        _outproj_kernel,
        out_shape=(jax.ShapeDtypeStruct((n_lat, d), F32),
                   jax.ShapeDtypeStruct((n_lat, d), F32),
                   jax.ShapeDtypeStruct((n_lat, LANES), jnp.int32),
                   jax.ShapeDtypeStruct((n_lat, LANES), F32),
                   jax.ShapeDtypeStruct((1, LANES), F32)),
        grid=(n_lat // tm,),
        in_specs=[row(CONV_CH), row(N_HEADS * V_DIM), row(d), _const_spec(wo.shape),
                  _const_spec(mod2.shape), _const_spec(g_ffn.shape), _const_spec(rw.shape),
                  _const_spec(rb.shape)],
        out_specs=(row(d), row(d), row(LANES), row(LANES), _const_spec((1, LANES))),
        scratch_shapes=[pltpu.VMEM((1, LANES), F32)],
        compiler_params=_cparams(("arbitrary",)),
        name="outproj",
    )(conv, attn, x, wo, mod2, g_ffn, rw, rb)


ROW_DMA_UNROLL = 8


def _fetch_rows(idx_first_ref, idx_next_ref, src_hbm, buf, sem):
    i = pl.program_id(0)
    n = pl.num_programs(0)
    n_rows = buf.shape[1]
    slot = lax.rem(i, 2)

    def row_copy(idx_ref, sl, r):
        return pltpu.make_async_copy(src_hbm.at[pl.ds(idx_ref[0, 0, r], 1)], buf.at[sl, pl.ds(r, 1)],
                                     sem.at[sl])

    def request(idx_ref, sl):
        def body(j, c):
            row_copy(idx_ref, sl, 2 * j).start(priority=0)
            row_copy(idx_ref, sl, 2 * j + 1).start(priority=1)
            return c
        lax.fori_loop(0, n_rows // 2, body, 0, unroll=ROW_DMA_UNROLL // 2)

    @pl.when(i == 0)
    def _():
        request(idx_first_ref, 0)

    @pl.when(i + 1 < n)
    def _():
        request(idx_next_ref, 1 - slot)

    def arrive(r, c):
        row_copy(idx_first_ref, slot, r).wait()
        return c
    lax.fori_loop(0, n_rows, arrive, 0, unroll=ROW_DMA_UNROLL)
    return slot


def _idx_specs(n_per_step, n_steps):
    first = pl.BlockSpec((1, 1, n_per_step), lambda i: (0, 0, 0), memory_space=pltpu.SMEM)
    nxt = pl.BlockSpec((1, 1, n_per_step), lambda i: (jnp.minimum(i + 1, n_steps - 1), 0, 0),
                       memory_space=pltpu.SMEM)
    return first, nxt


def _gather_kernel(tok_first_ref, tok_next_ref, h_hbm, o_ref, buf, sem):
    slot = _fetch_rows(tok_first_ref, tok_next_ref, h_hbm, buf, sem)
    o_ref[...] = buf[slot].astype(o_ref.dtype)


def _gather_rows(row_tok, h2, nb):
    n_rows = row_tok.shape[0]
    d = h2.shape[1]
    n_steps = n_rows // nb
    toks = row_tok.reshape(n_steps, 1, nb)
    first, nxt = _idx_specs(nb, n_steps)
    return pl.pallas_call(
        _gather_kernel,
        out_shape=jax.ShapeDtypeStruct((n_rows, d), BF16),
        grid=(n_steps,),
        in_specs=[first, nxt, pl.BlockSpec(memory_space=pl.ANY)],
        out_specs=pl.BlockSpec((nb, d), lambda i: (i, 0)),
        scratch_shapes=[pltpu.VMEM((2, nb, d), F32), pltpu.SemaphoreType.DMA((2,))],
        compiler_params=_cparams(("arbitrary",)),
        name="gather",
    )(toks, toks, h2)


SB_BLOCKS = 12
EXPERT_TF = 256
EXPERT_GROUP = 4


def _expert_kernel(sbe_ref, sbb0_ref, sbnb_ref, nsup_ref, nused_ref,
                   xs_hbm, wg_ref, wl_ref, wd_ref, bg_ref, bl_ref, bd_ref, ys_hbm,
                   xbuf, acc, wgl_s, wd_s, zbuf, xsem, ysem):
    s = pl.program_id(0)
    f = pl.program_id(1)
    nf = pl.num_programs(1)
    tf = wd_s.shape[0]
    n_sup = nsup_ref[0]
    n_blocks = ys_hbm.shape[0] // ROW_BLK

    def x_copy(sb, slot, b):
        row0 = pl.multiple_of((sbb0_ref[sb] + b) * ROW_BLK, ROW_BLK)
        return pltpu.make_async_copy(xs_hbm.at[pl.ds(row0, ROW_BLK)],
                                     xbuf.at[slot, pl.ds(pl.multiple_of(b * ROW_BLK, ROW_BLK), ROW_BLK)],
                                     xsem.at[slot])

    def y_copy(sb, b):
        row0 = pl.multiple_of((sbb0_ref[sb] + b) * ROW_BLK, ROW_BLK)
        return pltpu.make_async_copy(acc.at[pl.ds(pl.multiple_of(b * ROW_BLK, ROW_BLK), ROW_BLK)],
                                     ys_hbm.at[pl.ds(row0, ROW_BLK)], ysem)

    def z_copy(b):
        return pltpu.make_async_copy(zbuf, ys_hbm.at[pl.ds(pl.multiple_of(b * ROW_BLK, ROW_BLK), ROW_BLK)],
                                     ysem)

    def for_blocks(n, fn):
        def body(b, c):
            fn(b)
            return c
        lax.fori_loop(0, n, body, 0)

    @pl.when(s < n_sup)
    def _():
        slot = lax.rem(s, 2)
        nb = sbnb_ref[s]

        @pl.when(f == 0)
        def _():
            @pl.when(s == 0)
            def _():
                for_blocks(nb, lambda b: x_copy(0, 0, b).start())

            for_blocks(nb, lambda b: x_copy(s, slot, b).wait())

            @pl.when(s + 1 < n_sup)
            def _():
                nxt = jnp.minimum(s + 1, n_sup - 1)
                for_blocks(sbnb_ref[nxt], lambda b: x_copy(nxt, 1 - slot, b).start())

        wgl_s[:, :tf] = wg_ref[0].astype(BF16)
        wgl_s[:, tf:] = wl_ref[0].astype(BF16)
        wd_s[...] = wd_ref[0].astype(BF16)
        bg, bl = bg_ref[0], bl_ref[0]

        @pl.when(f == 0)
        def _():
            bias = jnp.broadcast_to(bd_ref[0], (ROW_BLK, acc.shape[1]))

            def init(b):
                acc[pl.ds(pl.multiple_of(b * ROW_BLK, ROW_BLK), ROW_BLK), :] = bias
            for_blocks(nb, init)

        def add_tile(b, m):
            rows = m * ROW_BLK
            r0 = pl.multiple_of(b * ROW_BLK, ROW_BLK)
            gu = jnp.dot(xbuf[slot, pl.ds(r0, rows), :], wgl_s[...], preferred_element_type=F32)
            glu = jnp.minimum(gu[:, :tf] + bg, SWIGLU_LIMIT)
            lin = jnp.clip(gu[:, tf:] + bl, -SWIGLU_LIMIT, SWIGLU_LIMIT)
            hmid = glu * jax.nn.sigmoid(SWIGLU_ALPHA * glu) * (lin + 1.0)
            acc[pl.ds(r0, rows), :] += jnp.dot(hmid.astype(BF16), wd_s[...], preferred_element_type=F32)

            @pl.when(f == nf - 1)
            def _():
                for u in range(m):
                    y_copy(s, b + u).start()

        n_grp = nb // EXPERT_GROUP
        for_blocks(n_grp, lambda g: add_tile(g * EXPERT_GROUP, EXPERT_GROUP))
        done = n_grp * EXPERT_GROUP
        m = EXPERT_GROUP // 2
        while m >= 1:
            take = ((nb - done) // m) > 0

            @pl.when(take)
            def _(m=m, done=done):
                add_tile(done, m)
            done = done + jnp.where(take, m, 0)
            m //= 2

        @pl.when(f == nf - 1)
        def _():
            for_blocks(nb, lambda b: y_copy(s, b).wait())

    @pl.when((s == pl.num_programs(0) - 1) & (f == nf - 1))
    def _():
        zbuf[...] = jnp.zeros_like(zbuf)
        n_used = nused_ref[0]

        def tail(fn):
            def body(b, c):
                fn(b)
                return c
            lax.fori_loop(n_used, n_blocks, body, 0)

        tail(lambda b: z_copy(b).start())
        tail(lambda b: z_copy(b).wait())


def _experts(sb_e, sb_b0, sb_nb, n_sup, n_used, xs, w_gate_up, b_gate_up, w_down, b_down):
    n_rows, d = xs.shape
    tf = EXPERT_TF
    de = w_down.shape[1]
    nf = de // tf
    n_e = w_down.shape[0]
    n_sb = sb_e.shape[0]
    rs = SB_BLOCKS * ROW_BLK
    bgu = b_gate_up.reshape(n_e, 1, 2 * de)
    bdn = b_down.reshape(n_e, 1, d)

    def e_of(s, nsup, sbe):
        return sbe[jnp.minimum(s, nsup[0] - 1)]

    def f_of(s, f, nsup):
        return jnp.where(s < nsup[0], f, nf - 1)

    grid_spec = pltpu.PrefetchScalarGridSpec(
        num_scalar_prefetch=5,
        grid=(n_sb, nf),
        in_specs=[
            pl.BlockSpec(memory_space=pl.ANY),
            pl.BlockSpec((1, d, tf), lambda s, f, sbe, b0, nb, ns, nu: (e_of(s, ns, sbe), 0, f_of(s, f, ns))),
            pl.BlockSpec((1, d, tf), lambda s, f, sbe, b0, nb, ns, nu: (e_of(s, ns, sbe), 0, nf + f_of(s, f, ns))),
            pl.BlockSpec((1, tf, d), lambda s, f, sbe, b0, nb, ns, nu: (e_of(s, ns, sbe), f_of(s, f, ns), 0)),
            pl.BlockSpec((1, 1, tf), lambda s, f, sbe, b0, nb, ns, nu: (e_of(s, ns, sbe), 0, f_of(s, f, ns))),
            pl.BlockSpec((1, 1, tf), lambda s, f, sbe, b0, nb, ns, nu: (e_of(s, ns, sbe), 0, nf + f_of(s, f, ns))),
            pl.BlockSpec((1, 1, d), lambda s, f, sbe, b0, nb, ns, nu: (e_of(s, ns, sbe), 0, 0)),
        ],
        out_specs=pl.BlockSpec(memory_space=pl.ANY),
        scratch_shapes=[pltpu.VMEM((2, rs, d), BF16), pltpu.VMEM((rs, d), F32),
                        pltpu.VMEM((d, 2 * tf), BF16), pltpu.VMEM((tf, d), BF16),
                        pltpu.VMEM((ROW_BLK, d), F32),
                        pltpu.SemaphoreType.DMA((2,)), pltpu.SemaphoreType.DMA],
    )
    return pl.pallas_call(
        _expert_kernel,
        out_shape=jax.ShapeDtypeStruct((n_rows, d), F32),
        grid_spec=grid_spec,
        compiler_params=_cparams(("arbitrary", "arbitrary")),
        name="experts",
    )(sb_e, sb_b0, sb_nb, n_sup, n_used, xs, w_gate_up, w_gate_up, w_down, bgu, bgu, bdn)


def _combine_kernel(dest_first_ref, dest_next_ref, y_hbm, x1_ref, gate_ref, g2_ref, o_ref, buf, sem):
    tm = x1_ref.shape[0]
    slot = _fetch_rows(dest_first_ref, dest_next_ref, y_hbm, buf, sem)
    gates = gate_ref[...]
    acc = buf[slot, 0:tm, :] * gates[:, 0:1]
    for kk in range(1, TOP_K):
        acc = acc + buf[slot, kk * tm:(kk + 1) * tm, :] * gates[:, kk:kk + 1]
    o_ref[...] = x1_ref[...] + g2_ref[...] * acc


def _combine(dest_km, yb, x1, gates, g2, tm):
    n_lat, d = x1.shape
    n_steps = n_lat // tm
    first, nxt = _idx_specs(TOP_K * tm, n_steps)
    return pl.pallas_call(
        _combine_kernel,
        out_shape=jax.ShapeDtypeStruct((n_lat, d), F32),
        grid=(n_steps,),
        in_specs=[first, nxt,
                  pl.BlockSpec(memory_space=pl.ANY),
                  pl.BlockSpec((tm, d), lambda i: (i, 0)),
                  pl.BlockSpec((tm, LANES), lambda i: (i, 0)),
                  _const_spec((1, d))],
        out_specs=pl.BlockSpec((tm, d), lambda i: (i, 0)),
        scratch_shapes=[pltpu.VMEM((2, TOP_K * tm, d), F32), pltpu.SemaphoreType.DMA((2,))],
        compiler_params=_cparams(("arbitrary",)),
        name="combine",
    )(dest_km, dest_km, yb, x1, gates, g2)


def _head_cols(w, per_head, start, width, perm=None):
    k = w.shape[0]
    t = w.reshape(k, N_HEADS, per_head)[:, :, start:start + width]
    return t if perm is None else t[:, :, jnp.asarray(perm)]


def _head_layout(nope, rope):
    k = nope.shape[0]
    parts = [nope]
    if rope is not None:
        parts.append(rope)
    used = sum(p.shape[-1] for p in parts)
    parts.append(jnp.zeros((k, N_HEADS, HEAD_W - used), nope.dtype))
    return jnp.concatenate(parts, axis=-1).reshape(k, N_HEADS * HEAD_W)


def _gain_row(g, mult=1.0):
    perm = jnp.asarray(_ROPE_PERM)
    return jnp.concatenate([g[:QK_NOPE], g[QK_NOPE:][perm], jnp.zeros((HEAD_W - QK_DIM,), F32)])[None] * mult


def _rope_tables(n_lat, n_ctx):
    rows = n_lat // GRID_W
    row = jnp.broadcast_to(jnp.arange(rows, dtype=F32)[:, None], (rows, GRID_W)).reshape(-1)
    col = jnp.broadcast_to(jnp.arange(GRID_W, dtype=F32)[None, :], (rows, GRID_W)).reshape(-1)
    inv_freq = ROPE_THETA ** (-2.0 * jnp.arange(AXIS_PAIRS, dtype=F32) / AXIS_ROPE)
    ang = jnp.concatenate([row[:, None] * inv_freq, col[:, None] * inv_freq], axis=1)
    cs, sn = jnp.cos(ang), jnp.sin(ang)
    one = jnp.ones((n_lat, QK_NOPE), F32)
    zn = jnp.zeros((n_lat, QK_NOPE), F32)
    zp = jnp.zeros((n_lat, HEAD_W - QK_DIM), F32)
    za = jnp.zeros((n_lat, AXIS_ROPE), F32)
    cos = jnp.concatenate([one, cs, cs, zp], axis=1)
    s_up = jnp.concatenate([zn, za, sn, zp], axis=1)
    s_dn = jnp.concatenate([zn, -sn, za, zp], axis=1)
    pad = lambda t, fill: jnp.concatenate([t, jnp.full((n_ctx, HEAD_W), fill, F32)], axis=0)
    return pad(cos, 1.0), pad(s_up, 0.0), pad(s_dn, 0.0)


def kernel(x, c, ctx, c_ctx, w_ada, b_ada, norm_mix, norm_ffn, w_in, conv_w, conv_b, conv_norm_g, conv_norm_b, q_a_norm, w_q_up, kv_a_norm, w_kv_up, q_head_norm, k_head_norm, w_out, router_w, router_b, w_gate_up, b_gate_up, w_down, b_down):
    bsz, n_lat, d = x.shape
    n_ctx = ctx.shape[1]
    assert bsz == 1 and w_ada.shape[0] == 1
    n_e = router_w.shape[-1]
    perm = _ROPE_PERM

    mod = _ada(jnp.stack([c[0], c_ctx], axis=1), w_ada[0], b_ada[0])
    mod = mod.reshape(2, N_MOD, d)
    mod1 = mod[:, 0:2, :]
    mod2 = jnp.stack([mod[0, 2], mod[0, 3], mod[0, 4]], axis=0)
    g2 = mod[0, 5][None]

    wi = w_in[0]
    kr_cols = jnp.concatenate([jnp.zeros((d, QK_NOPE), F32), wi[:, KR0:][:, jnp.asarray(perm)],
                               jnp.zeros((d, HEAD_W - QK_DIM), F32)], axis=1)
    win = jnp.concatenate([wi[:, :KR0], kr_cols], axis=1).astype(BF16)
    wq = _head_layout(_head_cols(w_q_up[0], QK_DIM, 0, QK_NOPE),
                      _head_cols(w_q_up[0], QK_DIM, QK_NOPE, QK_ROPE, perm)).astype(BF16)
    wk = _head_layout(_head_cols(w_kv_up[0], QK_NOPE + V_DIM, 0, QK_NOPE), None)
    wv = _head_layout(_head_cols(w_kv_up[0], QK_NOPE + V_DIM, QK_NOPE, V_DIM), None)
    wkv = jnp.concatenate([wk, wv], axis=1).astype(BF16)
    gq = _gain_row(q_head_norm[0], SCALE * math.log2(math.e))
    gk = _gain_row(k_head_norm[0])
    vone = jnp.zeros((N_HEADS, HEAD_W), F32).at[:, V_DIM].set(1.0).reshape(1, N_HEADS * HEAD_W)
    cos, s_up, s_dn = _rope_tables(n_lat, n_ctx)

    bound = (QK_DIM * SCALE * math.log2(math.e) * 1.01) * (jnp.max(jnp.abs(q_head_norm[0]))
                                                          * jnp.max(jnp.abs(k_head_norm[0])))
    shift = jnp.ceil(bound * 4.0) * 0.25
    fixed_ok = shift <= SHIFT_LIMIT
    lane = jnp.arange(HEAD_W)[None]
    qadd = jnp.where(lane == QK_DIM, jnp.where(fixed_ok, -shift, 0.0), 0.0).astype(F32)
    kadd = jnp.where(lane == QK_DIM, 1.0, 0.0).astype(F32)

    xin = jnp.concatenate([x[0], ctx[0]], axis=0)
    glu, q, k, vt = _inproj(xin, mod1, norm_mix, win, q_a_norm, wq, kv_a_norm, wkv, gq, gk, qadd, kadd, vone,
                            cos, s_up, s_dn, n_lat, tm=256)
    conv = _conv(glu, conv_w[0], conv_b, conv_norm_g, conv_norm_b, n_lat, tc=512)
    t_all = n_lat + n_ctx
    tk = max(t for t in range(256, 1025, 256) if t_all % t == 0)
    tq = min(2048, n_lat)
    attn = lax.cond(fixed_ok,
                    lambda: _attn(_attn_fixed_kernel, q, k, vt, n_lat, tq, tk),
                    lambda: _attn(_attn_online_kernel, q, k, vt, n_lat, tq, tk))

    rw = jnp.concatenate([router_w[0], jnp.zeros((d, LANES - n_e), F32)], axis=1)
    rb = jnp.concatenate([router_b[0], jnp.full((LANES - n_e,), -1e30, F32)])[None]
    x1, h2, meta, gates, cnt = _outproj(conv, attn, x[0], w_out[0].astype(BF16), mod2, norm_ffn, rw, rb,
                                        tm=256)

    out = _moe(x1, h2, meta, gates, cnt, g2, w_gate_up[0], b_gate_up[0], w_down[0], b_down[0])
    return out[None]


def _moe(x1, h2, meta, gates, cnt, g2, w_gate_up, b_gate_up, w_down, b_down):
    n_lat, d = h2.shape
    n_e = w_down.shape[0]
    counts = cnt[0, :n_e].astype(jnp.int32)
    top_idx = meta[:, :TOP_K]
    pos = meta[:, TOP_K:2 * TOP_K]
    nblk_e = (counts + ROW_BLK - 1) // ROW_BLK
    blk_end = jnp.cumsum(nblk_e)
    pad_start = (blk_end - nblk_e) * ROW_BLK
    dest = pad_start[top_idx] + pos
    n_blocks = (n_lat * TOP_K) // ROW_BLK + n_e
    n_rows = n_blocks * ROW_BLK
    tok = jnp.broadcast_to(jnp.arange(n_lat, dtype=jnp.int32)[:, None], (n_lat, TOP_K))
    row_tok = jnp.zeros((n_rows,), jnp.int32).at[dest.reshape(-1)].set(tok.reshape(-1))
    n_used = blk_end[-1:].astype(jnp.int32)
    nsb_e = (nblk_e + SB_BLOCKS - 1) // SB_BLOCKS
    sb_end = jnp.cumsum(nsb_e)
    n_sb = -(-n_blocks // SB_BLOCKS) + n_e
    j = jnp.arange(n_sb)
    sb_e = jnp.minimum(jnp.sum(sb_end[None, :] <= j[:, None], axis=1), n_e - 1).astype(jnp.int32)
    local = j - (sb_end - nsb_e)[sb_e]
    sb_b0 = ((blk_end - nblk_e)[sb_e] + local * SB_BLOCKS).astype(jnp.int32)
    sb_nb = jnp.clip(nblk_e[sb_e] - local * SB_BLOCKS, 0, SB_BLOCKS).astype(jnp.int32)
    n_sup = sb_end[-1:].astype(jnp.int32)
    sb_b0 = jnp.where(j < n_sup[0], sb_b0, 0)
    sb_nb = jnp.where(j < n_sup[0], sb_nb, 0)

    xb = _gather_rows(row_tok, h2, nb=512)
    yb = _experts(sb_e, sb_b0, sb_nb, n_sup, n_used, xb, w_gate_up, b_gate_up, w_down, b_down)

    tm_c = 128
    dest_km = dest.reshape(n_lat // tm_c, tm_c, TOP_K).transpose(0, 2, 1).reshape(n_lat // tm_c, 1,
                                                                                  TOP_K * tm_c)
    return _combine(dest_km, yb, x1, gates, g2, tm=tm_c)
```
